```python
import math
import jax
import jax.numpy as jnp
from jax import lax
import numpy as np

D_MODEL = 1024
BATCH = 32
SEQ = 2048
DEPTH = 4

N_META = 16
BLOCK = 128
EPS = 1e-6
D_INNER = 2 * D_MODEL
SSD_HEAD_DIM = 64
SSD_HEADS = D_INNER // SSD_HEAD_DIM
SSD_GROUPS = 8
SSD_HEADS_PER_GROUP = SSD_HEADS // SSD_GROUPS
SSD_STATE = 128
CONV_WIDTH = 4
CONV_DIM = D_INNER + 2 * SSD_GROUPS * SSD_STATE
ATTN_HEAD_DIM = 64
ATTN_Q_HEADS = D_MODEL // ATTN_HEAD_DIM
ATTN_KV_HEADS = ATTN_Q_HEADS // 4
ATTN_REP = ATTN_Q_HEADS // ATTN_KV_HEADS
D_ATTN = ATTN_Q_HEADS * ATTN_HEAD_DIM
D_KV = ATTN_KV_HEADS * ATTN_HEAD_DIM
WINDOW = 128
ROPE_THETA = 10000.0
N_BRANCH = 2
D_FF = 4 * D_MODEL
D_IN_PROJ = D_INNER + CONV_DIM + SSD_HEADS + D_ATTN + 2 * D_KV + N_BRANCH * D_MODEL

kernel_name = 'hybrid_ssd_swa_sink_gated_block'


def rms_norm(x, g):
    xf = x.astype(jnp.float32)
    xf = xf * lax.rsqrt(jnp.mean(xf * xf, axis=-1, keepdims=True) + EPS)
    return (xf * g.astype(jnp.float32)).astype(x.dtype)


def left_pad(t, pad):
    return jnp.pad(t, [(0, 0), (pad, 0)] + [(0, 0)] * (t.ndim - 2))


def segsum(a):
    t = a.shape[-1]
    cs = jnp.cumsum(a, axis=-1)
    diff = cs[..., :, None] - cs[..., None, :]
    mask = jnp.tril(jnp.ones((t, t), dtype=bool))
    return jnp.where(mask, diff, -jnp.inf)


def causal_depthwise_conv(u, w, bias):
    k, c = w.shape
    out = lax.conv_general_dilated(
        u, w[:, None, :].astype(u.dtype), window_strides=(1,), padding=[(k - 1, 0)],
        dimension_numbers=('NWC', 'WIO', 'NWC'), feature_group_count=c)
    return out + bias.astype(u.dtype)


def ssd_chunked(xs, dt, a_log, bm, cm):
    bsz, L, _, _ = xs.shape
    dtype = xs.dtype
    pad = (-L) % BLOCK
    xs, dt, bm, cm = left_pad(xs, pad), left_pad(dt, pad), left_pad(bm, pad), left_pad(cm, pad)
    nc = (L + pad) // BLOCK
    g, r, p, n = SSD_GROUPS, SSD_HEADS_PER_GROUP, SSD_HEAD_DIM, SSD_STATE
    a = -jnp.exp(a_log.astype(jnp.float32))
    adt = (dt.astype(jnp.float32) * a).reshape(bsz, nc, BLOCK, g, r)
    adt = jnp.transpose(adt, (0, 1, 3, 4, 2))
    a_cum = jnp.cumsum(adt, axis=-1)
    xdt = (xs * dt[..., None]).reshape(bsz, nc, BLOCK, g, r, p)
    bc = bm.reshape(bsz, nc, BLOCK, g, n)
    cc = cm.reshape(bsz, nc, BLOCK, g, n)
    lmat = jnp.exp(segsum(adt)).astype(dtype)
    cb = jnp.einsum('bclgn,bcsgn->bcgls', cc, bc)
    y_diag = jnp.einsum('bcgls,bcgrls,bcsgrp->bclgrp', cb, lmat, xdt)
    decay_states = jnp.exp(a_cum[..., -1:] - a_cum).astype(dtype)
    states = jnp.einsum('bclgn,bcgrl,bclgrp->bcgrpn', bc, decay_states, xdt)
    chunk_tot = jnp.pad(a_cum[..., -1], ((0, 0), (1, 0), (0, 0), (0, 0)))
    decay_chunk = jnp.exp(segsum(jnp.transpose(chunk_tot, (0, 2, 3, 1)))).astype(dtype)
    states = jnp.concatenate([jnp.zeros_like(states[:, :1]), states], axis=1)
    states = jnp.einsum('bgrzc,bcgrpn->bzgrpn', decay_chunk, states)[:, :-1]
    y_off = jnp.einsum('bclgn,bcgrpn,bcgrl->bclgrp', cc, states, jnp.exp(a_cum).astype(dtype))
    y = (y_diag + y_off).reshape(bsz, nc * BLOCK, SSD_HEADS, p)
    return y[:, pad:]


def ssd_mixer(z, xbc, dt_raw, conv_w, conv_b, dt_bias, a_log, d_skip, norm_g):
    bsz, L, _ = xbc.shape
    xbc = jax.nn.silu(causal_depthwise_conv(xbc, conv_w, conv_b))
    xs, bm, cm = jnp.split(xbc, [D_INNER, D_INNER + SSD_GROUPS * SSD_STATE], axis=-1)
    xs = xs.reshape(bsz, L, SSD_HEADS, SSD_HEAD_DIM)
    bm = bm.reshape(bsz, L, SSD_GROUPS, SSD_STATE)
    cm = cm.reshape(bsz, L, SSD_GROUPS, SSD_STATE)
    dt = jax.nn.softplus(dt_raw + dt_bias.astype(dt_raw.dtype))
    y = ssd_chunked(xs, dt, a_log, bm, cm) + xs * d_skip.astype(xs.dtype)[:, None]
    y = y.reshape(bsz, L, D_INNER) * jax.nn.silu(z)
    y = rms_norm(y.reshape(bsz, L, SSD_GROUPS, D_INNER // SSD_GROUPS),
                 norm_g.reshape(SSD_GROUPS, D_INNER // SSD_GROUPS))
    return y.reshape(bsz, L, D_INNER)


def rotary(t, pos):
    half = t.shape[-1] // 2
    inv_freq = ROPE_THETA ** (-jnp.arange(half, dtype=jnp.float32) / half)
    ang = pos.astype(jnp.float32)[:, None] * inv_freq[None, :]
    cos = jnp.cos(ang)[None, :, None, :].astype(t.dtype)
    sin = jnp.sin(ang)[None, :, None, :].astype(t.dtype)
    t1, t2 = t[..., :half], t[..., half:]
    return jnp.concatenate([t1 * cos - t2 * sin, t2 * cos + t1 * sin], axis=-1)


def sliding_window_attention(q, k, v, q_norm_g, k_norm_g, sinks):
    bsz, L = q.shape[:2]
    pos = jnp.arange(L)
    q = rotary(rms_norm(q, q_norm_g), pos)
    k = rotary(rms_norm(k, k_norm_g), pos)
    k_meta, v_meta = k[:, :N_META], v[:, :N_META]
    pad = (-L) % BLOCK
    nb = (L + pad) // BLOCK
    qb = left_pad(q, pad).reshape(bsz, nb, BLOCK, ATTN_KV_HEADS, ATTN_REP, ATTN_HEAD_DIM)
    kb = left_pad(k, pad).reshape(bsz, nb, BLOCK, ATTN_KV_HEADS, ATTN_HEAD_DIM)
    vb = left_pad(v, pad).reshape(bsz, nb, BLOCK, ATTN_KV_HEADS, ATTN_HEAD_DIM)

    def with_prev(t):
        prev = jnp.pad(t, [(0, 0), (1, 0)] + [(0, 0)] * (t.ndim - 2))[:, :-1]
        return jnp.concatenate([prev, t], axis=2)

    kb2, vb2 = with_prev(kb), with_prev(vb)
    blk_pos = (jnp.arange(nb * BLOCK) - pad).reshape(nb, BLOCK)
    k_pos = jnp.concatenate([blk_pos - BLOCK, blk_pos], axis=1)
    rel = blk_pos[:, :, None] - k_pos[:, None, :]
    band_mask = (rel >= 0) & (rel < WINDOW) & (k_pos[:, None, :] >= N_META)
    meta_mask = jnp.arange(N_META)[None, None, :] <= blk_pos[:, :, None]
    scale = ATTN_HEAD_DIM ** -0.5
    s_meta = jnp.einsum('bnqhrd,bmhd->bnhrqm', qb, k_meta).astype(jnp.float32) * scale
    s_band = jnp.einsum('bnqhrd,bnkhd->bnhrqk', qb, kb2).astype(jnp.float32) * scale
    s_meta = jnp.where(meta_mask[None, :, None, None], s_meta, -jnp.inf)
    s_band = jnp.where(band_mask[None, :, None, None], s_band, -jnp.inf)
    sink = jnp.broadcast_to(
        sinks.astype(jnp.float32).reshape(1, 1, ATTN_KV_HEADS, ATTN_REP, 1, 1),
        s_meta.shape[:-1] + (1,))
    probs = jax.nn.softmax(jnp.concatenate([s_meta, s_band, sink], axis=-1), axis=-1).astype(v.dtype)
    out = (jnp.einsum('bnhrqm,bmhd->bnqhrd', probs[..., :N_META], v_meta)
           + jnp.einsum('bnhrqk,bnkhd->bnqhrd', probs[..., N_META:N_META + 2 * BLOCK], vb2))
    return out.reshape(bsz, nb * BLOCK, D_ATTN)[:, pad:]


def setup_inputs(seed: int = 0) -> dict:
    key = jax.random.key(seed)
    ks = jax.random.split(key, 20)
    f32 = jnp.float32

    def normal(k, shape, scale):
        return jax.random.normal(k, shape, f32) * scale

    dt0 = jnp.exp(jax.random.uniform(ks[7], (DEPTH, SSD_HEADS), f32, math.log(1e-3), math.log(1e-1)))
    return {
        'x': normal(ks[0], (BATCH, SEQ, D_MODEL), 1.0),
        'meta_tokens': normal(ks[1], (N_META, D_MODEL), 1.0),
        'norm1_g': 1.0 + normal(ks[2], (DEPTH, D_MODEL), 0.02),
        'w_in': normal(ks[3], (DEPTH, D_MODEL, D_IN_PROJ), D_MODEL ** -0.5),
        'b_gate': normal(ks[4], (DEPTH, N_BRANCH * D_MODEL), 0.01),
        'conv_w': normal(ks[5], (DEPTH, CONV_WIDTH, CONV_DIM), CONV_WIDTH ** -0.5),
        'conv_b': normal(ks[6], (DEPTH, CONV_DIM), 0.01),
        'dt_bias': dt0 + jnp.log(-jnp.expm1(-dt0)),
        'a_log': jnp.log(jax.random.uniform(ks[8], (DEPTH, SSD_HEADS), f32, 1.0, 16.0)),
        'd_skip': 1.0 + normal(ks[9], (DEPTH, SSD_HEADS), 0.02),
        'ssd_norm_g': 1.0 + normal(ks[10], (DEPTH, D_INNER), 0.02),
        'q_norm_g': 1.0 + normal(ks[11], (DEPTH, ATTN_HEAD_DIM), 0.02),
        'k_norm_g': 1.0 + normal(ks[12], (DEPTH, ATTN_HEAD_DIM), 0.02),
        'sinks': normal(ks[13], (DEPTH, ATTN_Q_HEADS), 1.0),
        'w_ssd_down': normal(ks[14], (DEPTH, D_INNER, D_MODEL), D_INNER ** -0.5),
        'w_attn_down': normal(ks[15], (DEPTH, D_ATTN, D_MODEL), D_ATTN ** -0.5),
        'w_o': normal(ks[16], (DEPTH, D_MODEL, D_MODEL), D_MODEL ** -0.5),
        'norm2_g': 1.0 + normal(ks[17], (DEPTH, D_MODEL), 0.02),
        'w_mlp_up': normal(ks[18], (DEPTH, D_MODEL, D_FF), D_MODEL ** -0.5),
        'w_mlp_down': normal(ks[19], (DEPTH, D_FF, D_MODEL), D_FF ** -0.5),
    }


def reference(x, meta_tokens, norm1_g, w_in, b_gate, conv_w, conv_b, dt_bias, a_log, d_skip,
              ssd_norm_g, q_norm_g, k_norm_g, sinks, w_ssd_down, w_attn_down, w_o, norm2_g,
              w_mlp_up, w_mlp_down):
    bsz = x.shape[0]
    meta = jnp.broadcast_to(meta_tokens[None].astype(x.dtype), (bsz, N_META, D_MODEL))
    h = jnp.concatenate([meta, x], axis=1)
    L = h.shape[1]
    split_at = [D_INNER, D_INNER + CONV_DIM, D_INNER + CONV_DIM + SSD_HEADS,
                D_INNER + CONV_DIM + SSD_HEADS + D_ATTN,
                D_INNER + CONV_DIM + SSD_HEADS + D_ATTN + D_KV,
                D_INNER + CONV_DIM + SSD_HEADS + D_ATTN + 2 * D_KV]
    for l in range(DEPTH):
        xn = rms_norm(h, norm1_g[l])
        proj = xn @ w_in[l]
        z, xbc, dt_raw, q, k, v, gate_logits = jnp.split(proj, split_at, axis=-1)
        y_ssd = ssd_mixer(z, xbc, dt_raw, conv_w[l], conv_b[l], dt_bias[l], a_log[l],
                          d_skip[l], ssd_norm_g[l])
        y_attn = sliding_window_attention(
            q.reshape(bsz, L, ATTN_Q_HEADS, ATTN_HEAD_DIM),
            k.reshape(bsz, L, ATTN_KV_HEADS, ATTN_HEAD_DIM),
            v.reshape(bsz, L, ATTN_KV_HEADS, ATTN_HEAD_DIM),
            q_norm_g[l], k_norm_g[l], sinks[l])
        gates = jax.nn.sigmoid(gate_logits + b_gate[l].astype(gate_logits.dtype))
        g_ssd, g_attn = jnp.split(gates, 2, axis=-1)
        merged = g_ssd * (y_ssd @ w_ssd_down[l]) + g_attn * (y_attn @ w_attn_down[l])
        h = h + merged @ w_o[l]
        hn = rms_norm(h, norm2_g[l])
        h = h + jnp.square(jax.nn.relu(hn @ w_mlp_up[l])) @ w_mlp_down[l]
    return h[:, N_META:]
```

```python
import functools

import jax
import jax.numpy as jnp
from jax import lax
from jax.experimental import pallas as pl
from jax.experimental.pallas import tpu as pltpu

F32 = jnp.float32
BF16 = jnp.bfloat16

D_MODEL = 1024
N_META = 16
BLOCK = 128
EPS = 1e-6
D_INNER = 2048
SSD_HEAD_DIM = 64
SSD_HEADS = 32
SSD_GROUPS = 8
HEADS_PER_GROUP = 4
SSD_STATE = 128
CONV_WIDTH = 4
CONV_DIM = 4096
GROUP_WIDTH = D_INNER // SSD_GROUPS
ATTN_HEAD_DIM = 64
Q_HEADS = 16
KV_HEADS = 4
D_ATTN = 1024
D_KV = 256
ROPE_THETA = 10000.0
D_FF = 4096

PAD = BLOCK - N_META
LANES = 128

COL_XBC = 0
COL_Z = CONV_DIM
COL_GATE = COL_Z + D_INNER
COL_Q = COL_GATE + 2 * D_MODEL
COL_K = COL_Q + D_ATTN
COL_V = COL_K + D_KV
N_PROJ = COL_V + D_KV

PROJ_DTYPE = F32
VMEM_LIMIT = 56 * 1024 * 1024


def _params(*sem):
    return pltpu.CompilerParams(dimension_semantics=sem, vmem_limit_bytes=VMEM_LIMIT)


def _sigmoid(x):
    return 1.0 / (1.0 + jnp.exp(-x))


def _dot(a, b):
    return jnp.dot(a, b, preferred_element_type=F32)


def _dot_nt(a, b):
    return lax.dot_general(a, b, (((1,), (1,)), ((), ())), preferred_element_type=F32)


def _dot_hilo(a, b):
    hi = a.astype(BF16)
    lo = (a - hi.astype(F32)).astype(BF16)
    return _dot(hi, b) + _dot(lo, b)


def _inproj_kernel(h_ref, g_ref, w_ref, wdt_ref, proj_ref, dt_ref, xn_ref):
    @pl.when(pl.program_id(1) == 0)
    def _():
        x = h_ref[...]
        ms = jnp.mean(x * x, axis=-1, keepdims=True)
        xn = (x * lax.rsqrt(ms + EPS) * g_ref[...]).astype(BF16)
        xn_ref[...] = xn
        dt_ref[...] = _dot(xn, wdt_ref[...])

    proj_ref[...] = _dot(xn_ref[...], w_ref[...]).astype(proj_ref.dtype)


def _inproj(h, g, w, wdt, *, tm, tn):
    t = h.shape[0]
    return pl.pallas_call(
        _inproj_kernel,
        grid=(t // tm, N_PROJ // tn),
        in_specs=[
            pl.BlockSpec((tm, D_MODEL), lambda i, j: (i, 0)),
            pl.BlockSpec((1, D_MODEL), lambda i, j: (0, 0)),
            pl.BlockSpec((D_MODEL, tn), lambda i, j: (0, j)),
            pl.BlockSpec((D_MODEL, LANES), lambda i, j: (0, 0)),
        ],
        out_specs=[
            pl.BlockSpec((tm, tn), lambda i, j: (i, j)),
            pl.BlockSpec((tm, LANES), lambda i, j: (i, 0)),
        ],
        out_shape=[
            jax.ShapeDtypeStruct((t, N_PROJ), PROJ_DTYPE),
            jax.ShapeDtypeStruct((t, LANES), F32),
        ],
        scratch_shapes=[pltpu.VMEM((tm, D_MODEL), BF16)],
        compiler_params=_params("arbitrary", "arbitrary"),
        name="in_proj",
    )(h, g, w, wdt)


def _cumsum_rows(x):
    row = lax.broadcasted_iota(jnp.int32, x.shape, 0)
    shift = 1
    while shift < x.shape[0]:
        x = x + jnp.where(row >= shift, pltpu.roll(x, shift, axis=0), 0.0)
        shift *= 2
    return x


def _ssd_kernel(xbc_ref, z_ref, dt_ref, cw_ref, cb_ref, dtb_ref, alog_ref, dskip_ref, ng_ref,
                expand_ref, y_ref, xpad_ref, state_ref):
    n = pl.program_id(1)
    row = lax.broadcasted_iota(jnp.int32, (BLOCK, 1), 0)
    valid = jnp.logical_or(n > 0, row >= PAD)

    @pl.when(n == 0)
    def _():
        xpad_ref[0:8, :] = jnp.zeros((8, CONV_DIM), F32)
        state_ref[...] = jnp.zeros_like(state_ref)

    x_in = jnp.where(valid, xbc_ref[0].astype(F32), 0.0)
    xpad_ref[8:8 + BLOCK, :] = x_in
    acc = cb_ref[...] + cw_ref[CONV_WIDTH - 1:CONV_WIDTH, :] * x_in
    for s in range(1, CONV_WIDTH):
        acc = acc + cw_ref[CONV_WIDTH - 1 - s:CONV_WIDTH - s, :] * xpad_ref[8 - s:8 - s + BLOCK, :]
    xpad_ref[0:8, :] = x_in[BLOCK - 8:, :]
    xc = acc * _sigmoid(acc)
    xs = xc[:, :D_INNER]

    lane = lax.broadcasted_iota(jnp.int32, (BLOCK, LANES), 1)
    dtr = dt_ref[0] + dtb_ref[...]
    dt = jnp.maximum(dtr, 0.0) + jnp.log1p(jnp.exp(-jnp.abs(dtr)))
    dt = jnp.where(jnp.logical_and(valid, lane < SSD_HEADS), dt, 0.0)
    a_neg = jnp.where(lane[0:1] < SSD_HEADS, -jnp.exp(alog_ref[...]), 0.0)
    acum = _cumsum_rows(dt * a_neg)
    atot = acum[BLOCK - 1:BLOCK, :]
    acum_t = acum.T

    per_head = jnp.concatenate([dt, jnp.exp(acum), jnp.exp(atot - acum)], axis=0)
    wide = _dot_hilo(per_head, expand_ref[...])
    dt_w = wide[0:BLOCK]
    ea_w = wide[BLOCK:2 * BLOCK]
    dec_w = wide[2 * BLOCK:3 * BLOCK]
    xdt = xs * dt_w
    xdt_b = xdt.astype(BF16)
    xw_b = (xdt * dec_w).astype(BF16)
    chunk_decay = ea_w[BLOCK - 1:BLOCK, :]

    li = lax.broadcasted_iota(jnp.int32, (BLOCK, BLOCK), 0)
    si = lax.broadcasted_iota(jnp.int32, (BLOCK, BLOCK), 1)
    tril = li >= si
    lane_lo = lane < SSD_HEAD_DIM

    y_parts = []
    for g in range(SSD_GROUPS):
        c0 = g * GROUP_WIDTH
        bg = xc[:, D_INNER + g * SSD_STATE:D_INNER + (g + 1) * SSD_STATE]
        cg = xc[:, D_INNER + SSD_GROUPS * SSD_STATE + g * SSD_STATE:
                D_INNER + SSD_GROUPS * SSD_STATE + (g + 1) * SSD_STATE].astype(BF16)
        bg_t = bg.T.astype(BF16)
        cb = _dot(cg, bg_t)
        st_old = state_ref[:, c0:c0 + GROUP_WIDTH]
        y_off = _dot(cg, st_old.astype(BF16)) * ea_w[:, c0:c0 + GROUP_WIDTH]
        st_new = _dot(bg_t, xw_b[:, c0:c0 + GROUP_WIDTH])
        state_ref[:, c0:c0 + GROUP_WIDTH] = st_old * chunk_decay[:, c0:c0 + GROUP_WIDTH] + st_new
        for pair in range(HEADS_PER_GROUP // 2):
            ms = []
            for r in range(2):
                hd = g * HEADS_PER_GROUP + 2 * pair + r
                diff = acum[:, hd:hd + 1] - acum_t[hd:hd + 1, :]
                lm = jnp.exp(jnp.where(tril, diff, -jnp.inf))
                ms.append((cb * lm).astype(BF16))
            lhs = jnp.concatenate(ms, axis=1)
            xp = xdt_b[:, c0 + pair * LANES:c0 + (pair + 1) * LANES]
            zero = jnp.zeros_like(xp)
            rhs = jnp.concatenate([jnp.where(lane_lo, xp, zero), jnp.where(lane_lo, zero, xp)], axis=0)
            y_diag = _dot(lhs, rhs)
            y_parts.append(y_diag + y_off[:, pair * LANES:(pair + 1) * LANES])
    y = jnp.concatenate(y_parts, axis=1) + xs * dskip_ref[...]
    zf = z_ref[0].astype(F32)
    y = y * (zf * _sigmoid(zf))
    outs = []
    for g in range(SSD_GROUPS):
        yg = y[:, g * GROUP_WIDTH:(g + 1) * GROUP_WIDTH]
        ms = jnp.mean(yg * yg, axis=-1, keepdims=True)
        outs.append(yg * lax.rsqrt(ms + EPS))
    y_ref[0] = (jnp.concatenate(outs, axis=1) * ng_ref[...]).astype(y_ref.dtype)


def _ssd(proj, dt_raw, cw, cb, dtb, alog, dskip, ng, expand):
    bsz, lp, _ = proj.shape
    const = lambda shape: pl.BlockSpec(shape, lambda b, n: (0, 0))
    return pl.pallas_call(
        _ssd_kernel,
        grid=(bsz, lp // BLOCK),
        in_specs=[
            pl.BlockSpec((1, BLOCK, CONV_DIM), lambda b, n: (b, n, COL_XBC // CONV_DIM)),
            pl.BlockSpec((1, BLOCK, D_INNER), lambda b, n: (b, n, COL_Z // D_INNER)),
            pl.BlockSpec((1, BLOCK, LANES), lambda b, n: (b, n, 0)),
            const((CONV_WIDTH, CONV_DIM)),
            const((1, CONV_DIM)),
            const((1, LANES)),
            const((1, LANES)),
            const((1, D_INNER)),
            const((1, D_INNER)),
            const((LANES, D_INNER)),
        ],
        out_specs=pl.BlockSpec((1, BLOCK, D_INNER), lambda b, n: (b, n, 0)),
        out_shape=jax.ShapeDtypeStruct((bsz, lp, D_INNER), BF16),
        scratch_shapes=[
            pltpu.VMEM((8 + BLOCK, CONV_DIM), F32),
            pltpu.VMEM((SSD_STATE, D_INNER), F32),
        ],
        compiler_params=_params("arbitrary", "arbitrary"),
        name="ssd_mixer",
    )(proj, proj, dt_raw, cw, cb, dtb, alog, dskip, ng, expand)


def _attn_kernel(sinks_ref, q_ref, k_ref, v_ref, cos_ref, sin_ref, qg_ref, kg_ref, bd_ref, o_ref,
                 kprev_ref, vprev_ref, kmeta_ref, vmeta_ref):
    n = pl.program_id(1)
    lane = lax.broadcasted_iota(jnp.int32, (BLOCK, LANES), 1)
    lane_lo = lane < ATTN_HEAD_DIM
    first_half = (lane % ATTN_HEAD_DIM) < (ATTN_HEAD_DIM // 2)
    cos = cos_ref[...]
    sin = sin_ref[...]
    bd = bd_ref[...]

    def norm_rope(t, gain):
        ss = _dot_hilo(t * t, bd)
        t = t * lax.rsqrt(ss * (1.0 / ATTN_HEAD_DIM) + EPS) * gain
        partner = jnp.where(first_half, pltpu.roll(t, LANES - 32, axis=1), pltpu.roll(t, 32, axis=1))
        return t * cos + partner * sin

    @pl.when(n == 0)
    def _():
        kprev_ref[...] = jnp.zeros_like(kprev_ref)
        vprev_ref[...] = jnp.zeros_like(vprev_ref)

    kblk = k_ref[0].astype(F32)
    vblk = v_ref[0].astype(F32)
    k_rot = jnp.concatenate(
        [norm_rope(kblk[:, c * LANES:(c + 1) * LANES], kg_ref[...]) for c in range(D_KV // LANES)], axis=1)

    @pl.when(n == 0)
    def _():
        kmeta_ref[...] = k_rot[PAD:, :].astype(BF16)
        vmeta_ref[...] = vblk[PAD:, :].astype(BF16)

    qi = lax.broadcasted_iota(jnp.int32, (BLOCK, BLOCK), 0)
    ki = lax.broadcasted_iota(jnp.int32, (BLOCK, BLOCK), 1)
    cur_ok = jnp.logical_and(qi >= ki, n >= 1)
    prev_ok = jnp.logical_and(ki > qi, n >= 2)
    mi = lax.broadcasted_iota(jnp.int32, (BLOCK, N_META), 1)
    qrow = lax.broadcasted_iota(jnp.int32, (BLOCK, N_META), 0)
    meta_ok = jnp.logical_or(n >= 1, mi <= qrow - PAD)
    rep = Q_HEADS // KV_HEADS
    cur_ok4 = jnp.concatenate([cur_ok] * rep, axis=0)
    prev_ok4 = jnp.concatenate([prev_ok] * rep, axis=0)
    meta_ok4 = jnp.concatenate([meta_ok] * rep, axis=0)

    def dup(t, j):
        sw = pltpu.roll(t, ATTN_HEAD_DIM, axis=1)
        lo = lax.broadcasted_iota(jnp.int32, t.shape, 1) < ATTN_HEAD_DIM
        return jnp.where(lo, t, sw) if j % 2 == 0 else jnp.where(lo, sw, t)

    kcur_b = k_rot
    out_cols = []
    for j in range(KV_HEADS):
        c = j // 2
        csl = slice(c * LANES, (c + 1) * LANES)
        k_cur = dup(k_rot[:, csl], j).astype(BF16)
        k_prev = dup(kprev_ref[:, csl].astype(F32), j).astype(BF16)
        k_meta = dup(kmeta_ref[:, csl].astype(F32), j).astype(BF16)
        v_cur = dup(vblk[:, csl], j).astype(BF16)
        v_prev = dup(vprev_ref[:, csl].astype(F32), j).astype(BF16)
        v_meta = dup(vmeta_ref[:, csl].astype(F32), j).astype(BF16)
        rows = []
        sink_rows = []
        for qc in range(2):
            col = (2 * j + qc) * LANES
            qr = norm_rope(q_ref[0, :, col:col + LANES].astype(F32), qg_ref[...]) * (ATTN_HEAD_DIM ** -0.5)
            rows.append(jnp.where(lane_lo, qr, 0.0))
            rows.append(jnp.where(lane_lo, 0.0, qr))
            for r in range(2):
                sink_rows.append(jnp.full((BLOCK, 1), sinks_ref[rep * j + 2 * qc + r], F32))
        qs = jnp.concatenate(rows, axis=0).astype(BF16)
        sink = jnp.concatenate(sink_rows, axis=0)
        s_cur = jnp.where(cur_ok4, _dot_nt(qs, k_cur), -jnp.inf)
        s_prev = jnp.where(prev_ok4, _dot_nt(qs, k_prev), -jnp.inf)
        s_meta = jnp.where(meta_ok4, _dot_nt(qs, k_meta), -jnp.inf)
        m = jnp.maximum(jnp.maximum(jnp.max(s_cur, axis=-1, keepdims=True),
                                    jnp.max(s_prev, axis=-1, keepdims=True)),
                        jnp.maximum(jnp.max(s_meta, axis=-1, keepdims=True), sink))
        p_cur = jnp.exp(s_cur - m)
        p_prev = jnp.exp(s_prev - m)
        p_meta = jnp.exp(s_meta - m)
        denom = (jnp.sum(p_cur, axis=-1, keepdims=True) + jnp.sum(p_prev, axis=-1, keepdims=True)
                 + jnp.sum(p_meta, axis=-1, keepdims=True) + jnp.exp(sink - m))
        o = (_dot(p_cur.astype(BF16), v_cur) + _dot(p_prev.astype(BF16), v_prev)
             + _dot(p_meta.astype(BF16), v_meta)) / denom
        for qc in range(2):
            out_cols.append(jnp.where(lane_lo, o[(2 * qc) * BLOCK:(2 * qc + 1) * BLOCK],
                                      o[(2 * qc + 1) * BLOCK:(2 * qc + 2) * BLOCK]))
    o_ref[0] = jnp.concatenate(out_cols, axis=1).astype(o_ref.dtype)
    kprev_ref[...] = kcur_b.astype(BF16)
    vprev_ref[...] = vblk.astype(BF16)


def _attn(proj, sinks, cos_t, sin_t, qg, kg, bd):
    bsz, lp, _ = proj.shape
    const = lambda shape: pl.BlockSpec(shape, lambda b, n: (0, 0))
    return pl.pallas_call(
        _attn_kernel,
        grid=(bsz, lp // BLOCK),
        in_specs=[
            pl.BlockSpec(memory_space=pltpu.SMEM),
            pl.BlockSpec((1, BLOCK, D_ATTN), lambda b, n: (b, n, COL_Q // D_ATTN)),
            pl.BlockSpec((1, BLOCK, D_KV), lambda b, n: (b, n, COL_K // D_KV)),
            pl.BlockSpec((1, BLOCK, D_KV), lambda b, n: (b, n, COL_V // D_KV)),
            pl.BlockSpec((BLOCK, LANES), lambda b, n: (n, 0)),
            pl.BlockSpec((BLOCK, LANES), lambda b, n: (n, 0)),
            const((1, LANES)),
            const((1, LANES)),
            const((LANES, LANES)),
        ],
        out_specs=pl.BlockSpec((1, BLOCK, D_ATTN), lambda b, n: (b, n, 0)),
        out_shape=jax.ShapeDtypeStruct((bsz, lp, D_ATTN), BF16),
        scratch_shapes=[
            pltpu.VMEM((BLOCK, D_KV), BF16),
            pltpu.VMEM((BLOCK, D_KV), BF16),
            pltpu.VMEM((N_META, D_KV), BF16),
            pltpu.VMEM((N_META, D_KV), BF16),
        ],
        compiler_params=_params("arbitrary", "arbitrary"),
        name="swa_attention",
    )(sinks, proj, proj, proj, cos_t, sin_t, qg, kg, bd)


def _merge_kernel(h_ref, ys_ref, ya_ref, gl_ref, bg_ref, wsd_ref, wad_ref, wo_ref, o_ref):
    gl = gl_ref[...].astype(F32) + bg_ref[...]
    g_ssd = _sigmoid(gl[:, :D_MODEL])
    g_attn = _sigmoid(gl[:, D_MODEL:])
    merged = g_ssd * _dot(ys_ref[...], wsd_ref[...]) + g_attn * _dot(ya_ref[...], wad_ref[...])
    o_ref[...] = h_ref[...] + _dot(merged.astype(BF16), wo_ref[...])


def _merge(h, ys, ya, proj, bg, wsd, wad, wo, *, tm):
    t = h.shape[0]
    row = lambda w: pl.BlockSpec((tm, w), lambda i: (i, 0))
    const = lambda shape: pl.BlockSpec(shape, lambda i: (0, 0))
    return pl.pallas_call(
        _merge_kernel,
        grid=(t // tm,),
        in_specs=[
            row(D_MODEL), row(D_INNER), row(D_ATTN),
            pl.BlockSpec((tm, 2 * D_MODEL), lambda i: (i, COL_GATE // (2 * D_MODEL))),
            const((1, 2 * D_MODEL)),
            const((D_INNER, D_MODEL)), const((D_ATTN, D_MODEL)), const((D_MODEL, D_MODEL)),
        ],
        out_specs=row(D_MODEL),
        out_shape=jax.ShapeDtypeStruct((t, D_MODEL), F32),
        input_output_aliases={0: 0},
        compiler_params=_params("arbitrary"),
        name="merge_out_proj",
    )(h, ys, ya, proj, bg, wsd, wad, wo)


def _mlp_kernel(h_ref, g_ref, wup_ref, wdn_ref, o_ref, *, ff_chunk):
    x = h_ref[...]
    ms = jnp.mean(x * x, axis=-1, keepdims=True)
    xn = (x * lax.rsqrt(ms + EPS) * g_ref[...]).astype(BF16)
    acc = x
    for c in range(D_FF // ff_chunk):
        u = jnp.maximum(_dot(xn, wup_ref[:, c * ff_chunk:(c + 1) * ff_chunk]), 0.0)
        acc = acc + _dot((u * u).astype(BF16), wdn_ref[c * ff_chunk:(c + 1) * ff_chunk, :])
    o_ref[...] = acc


def _mlp(h, g, wup, wdn, *, tm, ff_chunk):
    t = h.shape[0]
    const = lambda shape: pl.BlockSpec(shape, lambda i: (0, 0))
    return pl.pallas_call(
        functools.partial(_mlp_kernel, ff_chunk=ff_chunk),
        grid=(t // tm,),
        in_specs=[
            pl.BlockSpec((tm, D_MODEL), lambda i: (i, 0)),
            const((1, D_MODEL)),
            const((D_MODEL, D_FF)),
            const((D_FF, D_MODEL)),
        ],
        out_specs=pl.BlockSpec((tm, D_MODEL), lambda i: (i, 0)),
        out_shape=jax.ShapeDtypeStruct((t, D_MODEL), F32),
        input_output_aliases={0: 0},
        compiler_params=_params("arbitrary"),
        name="mlp",
    )(h, g, wup, wdn)


def _rope_tables(lp):
    half = ATTN_HEAD_DIM // 2
    inv_freq = ROPE_THETA ** (-jnp.arange(half, dtype=F32) / half)
    pos = (jnp.arange(lp) - PAD).astype(F32)
    ang = pos[:, None] * inv_freq[None, :]
    cos = jnp.cos(ang)
    sin = jnp.sin(ang)
    cos_t = jnp.tile(cos, (1, LANES // half))
    sin_t = jnp.tile(jnp.concatenate([-sin, sin], axis=1), (1, LANES // ATTN_HEAD_DIM))
    return cos_t, sin_t


def _row_tile(t, target):
    tm = target
    while t % tm:
        tm //= 2
    return tm


def kernel(x, meta_tokens, norm1_g, w_in, b_gate, conv_w, conv_b, dt_bias, a_log, d_skip, ssd_norm_g,
           q_norm_g, k_norm_g, sinks, w_ssd_down, w_attn_down, w_o, norm2_g, w_mlp_up, w_mlp_down):
    bsz, seq, _ = x.shape
    depth = w_in.shape[0]
    assert seq % BLOCK == 0
    lp = seq + BLOCK
    t = bsz * lp

    meta = jnp.broadcast_to(meta_tokens[None].astype(x.dtype), (bsz, N_META, D_MODEL))
    h = jnp.concatenate([jnp.zeros((bsz, PAD, D_MODEL), x.dtype), meta, x], axis=1).reshape(t, D_MODEL)

    cos_t, sin_t = _rope_tables(lp)
    expand = (jnp.arange(LANES)[:, None] == (jnp.arange(D_INNER)[None, :] // SSD_HEAD_DIM)).astype(BF16)
    bd = ((jnp.arange(LANES)[:, None] // ATTN_HEAD_DIM)
          == (jnp.arange(LANES)[None, :] // ATTN_HEAD_DIM)).astype(BF16)

    o_z, o_xbc, o_dt = 0, D_INNER, D_INNER + CONV_DIM
    o_q = o_dt + SSD_HEADS
    o_k, o_v, o_g = o_q + D_ATTN, o_q + D_ATTN + D_KV, o_q + D_ATTN + 2 * D_KV

    tm_proj = _row_tile(t, 1024)
    tm_rows = _row_tile(t, 512)

    for l in range(depth):
        w = w_in[l]
        w_main = jnp.concatenate(
            [w[:, o_xbc:o_dt], w[:, o_z:o_xbc], w[:, o_g:], w[:, o_q:o_k], w[:, o_k:o_v], w[:, o_v:o_g]],
            axis=1).astype(BF16)
        w_dt = jnp.pad(w[:, o_dt:o_q], ((0, 0), (0, LANES - SSD_HEADS))).astype(BF16)
        proj, dt_raw = _inproj(h, norm1_g[l][None], w_main, w_dt, tm=tm_proj, tn=N_PROJ // 4)
        proj3 = proj.reshape(bsz, lp, N_PROJ)

        pad_heads = lambda v: jnp.pad(v, (0, LANES - SSD_HEADS))[None]
        y_ssd = _ssd(proj3, dt_raw.reshape(bsz, lp, LANES), conv_w[l], conv_b[l][None],
                     pad_heads(dt_bias[l]), pad_heads(a_log[l]),
                     jnp.repeat(d_skip[l], SSD_HEAD_DIM)[None], ssd_norm_g[l][None], expand)
        y_attn = _attn(proj3, sinks[l], cos_t, sin_t,
                       jnp.tile(q_norm_g[l], LANES // ATTN_HEAD_DIM)[None],
                       jnp.tile(k_norm_g[l], LANES // ATTN_HEAD_DIM)[None], bd)

        h = _merge(h, y_ssd.reshape(t, D_INNER), y_attn.reshape(t, D_ATTN), proj, b_gate[l][None],
                   w_ssd_down[l].astype(BF16), w_attn_down[l].astype(BF16), w_o[l].astype(BF16), tm=tm_rows)
        h = _mlp(h, norm2_g[l][None], w_mlp_up[l].astype(BF16), w_mlp_down[l].astype(BF16),
                 tm=tm_rows, ff_chunk=1024)

    return h.reshape(bsz, lp, D_MODEL)[:, BLOCK:]
```

```python
import functools

import jax
import jax.numpy as jnp
from jax import lax
from jax.experimental import pallas as pl
from jax.experimental.pallas import tpu as pltpu

F32 = jnp.float32
BF16 = jnp.bfloat16

D_MODEL = 1024
N_META = 16
BLOCK = 128
EPS = 1e-6
D_INNER = 2048
SSD_HEAD_DIM = 64
SSD_HEADS = 32
SSD_GROUPS = 8
HEADS_PER_GROUP = 4
SSD_STATE = 128
CONV_WIDTH = 4
CONV_DIM = 4096
GROUP_WIDTH = D_INNER // SSD_GROUPS
ATTN_HEAD_DIM = 64
Q_HEADS = 16
KV_HEADS = 4
D_ATTN = 1024
D_KV = 256
ROPE_THETA = 10000.0
D_FF = 4096

PAD = BLOCK - N_META
LANES = 128

COL_XBC = 0
COL_Z = CONV_DIM
COL_GATE = COL_Z + D_INNER
COL_Q = COL_GATE + 2 * D_MODEL
COL_K = COL_Q + D_ATTN
COL_V = COL_K + D_KV
N_PROJ = COL_V + D_KV

PROJ_DTYPE = BF16
VMEM_LIMIT = 56 * 1024 * 1024
LOG2E = 1.4426950408889634


def _params(*sem):
    return pltpu.CompilerParams(dimension_semantics=sem, vmem_limit_bytes=VMEM_LIMIT)


def _sigmoid(x):
    return 1.0 / (1.0 + jnp.exp2(x * (-LOG2E)))


def _dot(a, b):
    return jnp.dot(a, b, preferred_element_type=F32)


def _dot_nt(a, b):
    return lax.dot_general(a, b, (((1,), (1,)), ((), ())), preferred_element_type=F32)


def _dot_hilo(a, b):
    hi = a.astype(BF16)
    lo = (a - hi.astype(F32)).astype(BF16)
    return _dot(hi, b) + _dot(lo, b)


def _inproj_kernel(h_ref, g_ref, w_ref, wdt_ref, proj_ref, dt_ref, xn_ref):
    @pl.when(pl.program_id(1) == 0)
    def _():
        x = h_ref[...]
        ms = jnp.mean(x * x, axis=-1, keepdims=True)
        xn = (x * lax.rsqrt(ms + EPS) * g_ref[...]).astype(BF16)
        xn_ref[...] = xn
        dt_ref[...] = _dot(xn, wdt_ref[...])

    proj_ref[...] = _dot(xn_ref[...], w_ref[...]).astype(proj_ref.dtype)


def _inproj(h, g, w, wdt, *, tm, tn):
    t = h.shape[0]
    return pl.pallas_call(
        _inproj_kernel,
        grid=(t // tm, N_PROJ // tn),
        in_specs=[
            pl.BlockSpec((tm, D_MODEL), lambda i, j: (i, 0)),
            pl.BlockSpec((1, D_MODEL), lambda i, j: (0, 0)),
            pl.BlockSpec((D_MODEL, tn), lambda i, j: (0, j)),
            pl.BlockSpec((D_MODEL, LANES), lambda i, j: (0, 0)),
        ],
        out_specs=[
            pl.BlockSpec((tm, tn), lambda i, j: (i, j)),
            pl.BlockSpec((tm, LANES), lambda i, j: (i, 0)),
        ],
        out_shape=[
            jax.ShapeDtypeStruct((t, N_PROJ), PROJ_DTYPE),
            jax.ShapeDtypeStruct((t, LANES), F32),
        ],
        scratch_shapes=[pltpu.VMEM((tm, D_MODEL), BF16)],
        compiler_params=_params("arbitrary", "arbitrary"),
        name="in_proj",
    )(h, g, w, wdt)


def _cumsum_rows(x):
    row = lax.broadcasted_iota(jnp.int32, x.shape, 0)
    shift = 1
    while shift < x.shape[0]:
        x = x + jnp.where(row >= shift, pltpu.roll(x, shift, axis=0), 0.0)
        shift *= 2
    return x


def _ssd_kernel(xbc_ref, xprev_ref, z_ref, dt_ref, shift_ref, cw_ref, cb_ref, dtb_ref, alog_ref, dskip_ref, ng_ref,
                expand_ref, y_ref, state_ref):
    n = pl.program_id(1)
    row = lax.broadcasted_iota(jnp.int32, (BLOCK, 1), 0)
    valid = jnp.logical_or(n > 0, row >= PAD)

    @pl.when(n == 0)
    def _():
        state_ref[...] = jnp.zeros_like(state_ref)

    x_cur = xbc_ref[0]
    shifted = _dot(shift_ref[0], jnp.concatenate([xprev_ref[0], x_cur], axis=0))
    acc = cb_ref[...] + cw_ref[CONV_WIDTH - 1:CONV_WIDTH, :] * x_cur.astype(F32)
    for s in range(1, CONV_WIDTH):
        acc = acc + cw_ref[CONV_WIDTH - 1 - s:CONV_WIDTH - s, :] * shifted[(s - 1) * BLOCK:s * BLOCK]
    xc = acc * _sigmoid(acc)
    xs = xc[:, :D_INNER]

    lane = lax.broadcasted_iota(jnp.int32, (BLOCK, LANES), 1)
    dtr = dt_ref[0] + dtb_ref[...]
    dt = jnp.maximum(dtr, 0.0) + jnp.log1p(jnp.exp(-jnp.abs(dtr)))
    dt = jnp.where(jnp.logical_and(valid, lane < SSD_HEADS), dt, 0.0)
    a_neg = jnp.where(lane[0:1] < SSD_HEADS, -LOG2E * jnp.exp(alog_ref[...]), 0.0)
    acum = _cumsum_rows(dt * a_neg)
    atot = acum[BLOCK - 1:BLOCK, :]
    acum_t = acum.T

    per_head = jnp.concatenate([dt, jnp.exp2(acum), jnp.exp2(atot - acum)], axis=0)
    wide = _dot_hilo(per_head, expand_ref[...])
    dt_w = wide[0:BLOCK]
    ea_w = wide[BLOCK:2 * BLOCK]
    dec_w = wide[2 * BLOCK:3 * BLOCK]
    xdt = xs * dt_w
    xdt_b = xdt.astype(BF16)
    xw_b = (xdt * dec_w).astype(BF16)
    chunk_decay = ea_w[BLOCK - 1:BLOCK, :]

    li = lax.broadcasted_iota(jnp.int32, (BLOCK, BLOCK), 0)
    si = lax.broadcasted_iota(jnp.int32, (BLOCK, BLOCK), 1)
    tril = li >= si
    lane_lo = lane < SSD_HEAD_DIM

    y_parts = []
    for g in range(SSD_GROUPS):
        c0 = g * GROUP_WIDTH
        bg = xc[:, D_INNER + g * SSD_STATE:D_INNER + (g + 1) * SSD_STATE]
        cg = xc[:, D_INNER + SSD_GROUPS * SSD_STATE + g * SSD_STATE:
                D_INNER + SSD_GROUPS * SSD_STATE + (g + 1) * SSD_STATE].astype(BF16)
        bg_t = bg.T.astype(BF16)
        cb = _dot(cg, bg_t)
        st_old = state_ref[:, c0:c0 + GROUP_WIDTH]
        y_off = _dot(cg, st_old.astype(BF16)) * ea_w[:, c0:c0 + GROUP_WIDTH]
        st_new = _dot(bg_t, xw_b[:, c0:c0 + GROUP_WIDTH])
        state_ref[:, c0:c0 + GROUP_WIDTH] = st_old * chunk_decay[:, c0:c0 + GROUP_WIDTH] + st_new
        for pair in range(HEADS_PER_GROUP // 2):
            ms = []
            for r in range(2):
                hd = g * HEADS_PER_GROUP + 2 * pair + r
                diff = acum[:, hd:hd + 1] - acum_t[hd:hd + 1, :]
                lm = jnp.exp2(jnp.where(tril, diff, -jnp.inf))
                ms.append((cb * lm).astype(BF16))
            lhs = jnp.concatenate(ms, axis=1)
            xp = xdt_b[:, c0 + pair * LANES:c0 + (pair + 1) * LANES]
            zero = jnp.zeros_like(xp)
            rhs = jnp.concatenate([jnp.where(lane_lo, xp, zero), jnp.where(lane_lo, zero, xp)], axis=0)
            y_diag = _dot(lhs, rhs)
            y_parts.append(y_diag + y_off[:, pair * LANES:(pair + 1) * LANES])
    y = jnp.concatenate(y_parts, axis=1) + xs * dskip_ref[...]
    zf = z_ref[0].astype(F32)
    y = y * (zf * _sigmoid(zf))
    outs = []
    for g in range(SSD_GROUPS):
        yg = y[:, g * GROUP_WIDTH:(g + 1) * GROUP_WIDTH]
        ms = jnp.mean(yg * yg, axis=-1, keepdims=True)
        outs.append(yg * lax.rsqrt(ms + EPS))
    y_ref[0] = (jnp.concatenate(outs, axis=1) * ng_ref[...]).astype(y_ref.dtype)


def _ssd(proj, dt_raw, shift, cw, cb, dtb, alog, dskip, ng, expand):
    bsz, lp, _ = proj.shape
    const = lambda shape: pl.BlockSpec(shape, lambda b, n: (0, 0))
    return pl.pallas_call(
        _ssd_kernel,
        grid=(bsz, lp // BLOCK),
        in_specs=[
            pl.BlockSpec((1, BLOCK, CONV_DIM), lambda b, n: (b, n, COL_XBC // CONV_DIM)),
            pl.BlockSpec((1, BLOCK, CONV_DIM), lambda b, n: (b, jnp.maximum(n - 1, 0), COL_XBC // CONV_DIM)),
            pl.BlockSpec((1, BLOCK, D_INNER), lambda b, n: (b, n, COL_Z // D_INNER)),
            pl.BlockSpec((1, BLOCK, LANES), lambda b, n: (b, n, 0)),
            pl.BlockSpec((1, (CONV_WIDTH - 1) * BLOCK, 2 * BLOCK), lambda b, n: (jnp.minimum(n, 1), 0, 0)),
            const((CONV_WIDTH, CONV_DIM)),
            const((1, CONV_DIM)),
            const((1, LANES)),
            const((1, LANES)),
            const((1, D_INNER)),
            const((1, D_INNER)),
            const((LANES, D_INNER)),
        ],
        out_specs=pl.BlockSpec((1, BLOCK, D_INNER), lambda b, n: (b, n, 0)),
        out_shape=jax.ShapeDtypeStruct((bsz, lp, D_INNER), BF16),
        scratch_shapes=[pltpu.VMEM((SSD_STATE, D_INNER), F32)],
        compiler_params=_params("arbitrary", "arbitrary"),
        name="ssd_mixer",
    )(proj, proj, proj, dt_raw, shift, cw, cb, dtb, alog, dskip, ng, expand)


def _attn_kernel(sinks_ref, q_ref, k_ref, v_ref, cos_ref, sin_ref, qg_ref, kg_ref, bd_ref, o_ref,
                 kprev_ref, vprev_ref, kmeta_ref, vmeta_ref):
    n = pl.program_id(1)
    rep = Q_HEADS // KV_HEADS
    wide = rep * BLOCK
    lane = lax.broadcasted_iota(jnp.int32, (BLOCK, LANES), 1)
    row = lax.broadcasted_iota(jnp.int32, (BLOCK, LANES), 0)
    lane_lo = lane < ATTN_HEAD_DIM
    row_lo = row < ATTN_HEAD_DIM
    first_half = (lane % ATTN_HEAD_DIM) < (ATTN_HEAD_DIM // 2)
    cos = cos_ref[...]
    sin = sin_ref[...]
    bd = bd_ref[...]

    def norm_rope(t, gain):
        ms = _dot_hilo(t * t, bd)
        t = t * lax.rsqrt(ms + EPS) * gain
        partner = jnp.where(first_half, pltpu.roll(t, LANES - 32, axis=1), pltpu.roll(t, 32, axis=1))
        return t * cos + partner * sin

    kd_cur, vt_cur = [], []
    k_gain = kg_ref[...] * (ATTN_HEAD_DIM ** -0.5)
    for c in range(D_KV // LANES):
        csl = slice(c * LANES, (c + 1) * LANES)
        kr = norm_rope(k_ref[0, :, csl].astype(F32), k_gain)
        sw = pltpu.roll(kr, ATTN_HEAD_DIM, axis=1)
        kd_cur += [jnp.where(lane_lo, kr, sw).astype(BF16), jnp.where(lane_lo, sw, kr).astype(BF16)]
        vt = v_ref[0, :, csl].astype(F32).T
        lo, hi = vt[:ATTN_HEAD_DIM], vt[ATTN_HEAD_DIM:]
        vt_cur += [jnp.concatenate([lo, lo], axis=0).astype(BF16), jnp.concatenate([hi, hi], axis=0).astype(BF16)]

    @pl.when(n == 0)
    def _():
        kprev_ref[...] = jnp.zeros_like(kprev_ref)
        vprev_ref[...] = jnp.zeros_like(vprev_ref)
        for j in range(KV_HEADS):
            kmeta_ref[j] = kd_cur[j][PAD:, :]
            vmeta_ref[j] = vt_cur[j]

    neg = -jnp.inf
    key_i = lax.broadcasted_iota(jnp.int32, (BLOCK, wide), 0)
    qry_i = lax.broadcasted_iota(jnp.int32, (BLOCK, wide), 1) % BLOCK
    tri = key_i <= qry_i
    band_bias = jnp.where(tri, jnp.where(n >= 1, 0.0, neg), jnp.where(n >= 2, 0.0, neg))
    meta_key = lax.broadcasted_iota(jnp.int32, (N_META, wide), 0)
    meta_qry = lax.broadcasted_iota(jnp.int32, (N_META, wide), 1) % BLOCK
    meta_ok = jnp.logical_or(n >= 1, meta_key <= meta_qry - PAD)
    meta_bias = jnp.where(meta_ok, 0.0, neg)
    zeros_pad = jnp.zeros((PAD, wide), BF16)

    qs, sink = [], []
    for j in range(KV_HEADS):
        rows, sk = [], []
        for qc in range(2):
            col = (2 * j + qc) * LANES
            qr = norm_rope(q_ref[0, :, col:col + LANES].astype(F32), qg_ref[...])
            rows.append(jnp.where(lane_lo, qr, 0.0))
            rows.append(jnp.where(lane_lo, 0.0, qr))
            for r in range(2):
                sk.append(jnp.full((1, BLOCK), sinks_ref[rep * j + 2 * qc + r], F32))
        qs.append(jnp.concatenate(rows, axis=0).astype(BF16))
        sink.append(jnp.concatenate(sk, axis=1))
    s_cur = [_dot_nt(kd_cur[j], qs[j]) for j in range(KV_HEADS)]
    s_prev = [_dot_nt(kprev_ref[j], qs[j]) for j in range(KV_HEADS)]
    s_meta = [_dot_nt(kmeta_ref[j], qs[j]) + meta_bias for j in range(KV_HEADS)]
    s_band = [jnp.where(tri, s_cur[j], s_prev[j]) + band_bias for j in range(KV_HEADS)]
    m = [jnp.maximum(jnp.maximum(jnp.max(s_band[j], axis=0, keepdims=True),
                                 jnp.max(s_meta[j], axis=0, keepdims=True)), sink[j]) for j in range(KV_HEADS)]
    p_band = [jnp.exp(s_band[j] - m[j]) for j in range(KV_HEADS)]
    p_meta = [jnp.exp(s_meta[j] - m[j]) for j in range(KV_HEADS)]
    denom = [jnp.sum(p_band[j], axis=0, keepdims=True) + jnp.sum(p_meta[j], axis=0, keepdims=True)
             + jnp.exp(sink[j] - m[j]) for j in range(KV_HEADS)]
    p_cur = [jnp.where(tri, p_band[j], 0.0) for j in range(KV_HEADS)]
    p_all = [jnp.concatenate([p_cur[j].astype(BF16), (p_band[j] - p_cur[j]).astype(BF16), zeros_pad,
                              p_meta[j].astype(BF16)], axis=0) for j in range(KV_HEADS)]
    o_t = [_dot(jnp.concatenate([vt_cur[j], vprev_ref[j], vmeta_ref[j]], axis=1), p_all[j]) * (1.0 / denom[j])
           for j in range(KV_HEADS)]
    out_cols = []
    for j in range(KV_HEADS):
        for qc in range(2):
            pair = jnp.where(row_lo, o_t[j][:, (2 * qc) * BLOCK:(2 * qc + 1) * BLOCK],
                             o_t[j][:, (2 * qc + 1) * BLOCK:(2 * qc + 2) * BLOCK])
            out_cols.append(pair.T)
    o_ref[0] = jnp.concatenate(out_cols, axis=1).astype(o_ref.dtype)
    for j in range(KV_HEADS):
        kprev_ref[j] = kd_cur[j]
        vprev_ref[j] = vt_cur[j]


def _attn(proj, sinks, cos_t, sin_t, qg, kg, bd):
    bsz, lp, _ = proj.shape
    const = lambda shape: pl.BlockSpec(shape, lambda b, n: (0, 0))
    return pl.pallas_call(
        _attn_kernel,
        grid=(bsz, lp // BLOCK),
        in_specs=[
            pl.BlockSpec(memory_space=pltpu.SMEM),
            pl.BlockSpec((1, BLOCK, D_ATTN), lambda b, n: (b, n, COL_Q // D_ATTN)),
            pl.BlockSpec((1, BLOCK, D_KV), lambda b, n: (b, n, COL_K // D_KV)),
            pl.BlockSpec((1, BLOCK, D_KV), lambda b, n: (b, n, COL_V // D_KV)),
            pl.BlockSpec((BLOCK, LANES), lambda b, n: (n, 0)),
            pl.BlockSpec((BLOCK, LANES), lambda b, n: (n, 0)),
            const((1, LANES)),
            const((1, LANES)),
            const((LANES, LANES)),
        ],
        out_specs=pl.BlockSpec((1, BLOCK, D_ATTN), lambda b, n: (b, n, 0)),
        out_shape=jax.ShapeDtypeStruct((bsz, lp, D_ATTN), BF16),
        scratch_shapes=[
            pltpu.VMEM((KV_HEADS, BLOCK, LANES), BF16),
            pltpu.VMEM((KV_HEADS, LANES, BLOCK), BF16),
            pltpu.VMEM((KV_HEADS, N_META, LANES), BF16),
            pltpu.VMEM((KV_HEADS, LANES, BLOCK), BF16),
        ],
        compiler_params=_params("arbitrary", "arbitrary"),
        name="swa_attention",
    )(sinks, proj, proj, proj, cos_t, sin_t, qg, kg, bd)


def _merge_kernel(h_ref, ys_ref, ya_ref, gl_ref, bg_ref, wsd_ref, wad_ref, wo_ref, o_ref):
    gl = gl_ref[...].astype(F32) + bg_ref[...]
    g_ssd = _sigmoid(gl[:, :D_MODEL])
    g_attn = _sigmoid(gl[:, D_MODEL:])
    merged = g_ssd * _dot(ys_ref[...], wsd_ref[...]) + g_attn * _dot(ya_ref[...], wad_ref[...])
    o_ref[...] = h_ref[...] + _dot(merged.astype(BF16), wo_ref[...])


def _merge(h, ys, ya, proj, bg, wsd, wad, wo, *, tm):
    t = h.shape[0]
    row = lambda w: pl.BlockSpec((tm, w), lambda i: (i, 0))
    const = lambda shape: pl.BlockSpec(shape, lambda i: (0, 0))
    return pl.pallas_call(
        _merge_kernel,
        grid=(t // tm,),
        in_specs=[
            row(D_MODEL), row(D_INNER), row(D_ATTN),
            pl.BlockSpec((tm, 2 * D_MODEL), lambda i: (i, COL_GATE // (2 * D_MODEL))),
            const((1, 2 * D_MODEL)),
            const((D_INNER, D_MODEL)), const((D_ATTN, D_MODEL)), const((D_MODEL, D_MODEL)),
        ],
        out_specs=row(D_MODEL),
        out_shape=jax.ShapeDtypeStruct((t, D_MODEL), F32),
        input_output_aliases={0: 0},
        compiler_params=_params("arbitrary"),
        name="merge_out_proj",
    )(h, ys, ya, proj, bg, wsd, wad, wo)


def _mlp_kernel(h_ref, g_ref, wup_ref, wdn_ref, o_ref, *, ff_chunk):
    x = h_ref[...]
    ms = jnp.mean(x * x, axis=-1, keepdims=True)
    xn = (x * lax.rsqrt(ms + EPS) * g_ref[...]).astype(BF16)
    acc = x
    for c in range(D_FF // ff_chunk):
        u = jnp.maximum(_dot(xn, wup_ref[:, c * ff_chunk:(c + 1) * ff_chunk]), 0.0)
        acc = acc + _dot((u * u).astype(BF16), wdn_ref[c * ff_chunk:(c + 1) * ff_chunk, :])
    o_ref[...] = acc


def _mlp(h, g, wup, wdn, *, tm, ff_chunk):
    t = h.shape[0]
    const = lambda shape: pl.BlockSpec(shape, lambda i: (0, 0))
    return pl.pallas_call(
        functools.partial(_mlp_kernel, ff_chunk=ff_chunk),
        grid=(t // tm,),
        in_specs=[
            pl.BlockSpec((tm, D_MODEL), lambda i: (i, 0)),
            const((1, D_MODEL)),
            const((D_MODEL, D_FF)),
            const((D_FF, D_MODEL)),
        ],
        out_specs=pl.BlockSpec((tm, D_MODEL), lambda i: (i, 0)),
        out_shape=jax.ShapeDtypeStruct((t, D_MODEL), F32),
        input_output_aliases={0: 0},
        compiler_params=_params("arbitrary"),
        name="mlp",
    )(h, g, wup, wdn)


def _rope_tables(lp):
    half = ATTN_HEAD_DIM // 2
    inv_freq = ROPE_THETA ** (-jnp.arange(half, dtype=F32) / half)
    pos = (jnp.arange(lp) - PAD).astype(F32)
    ang = pos[:, None] * inv_freq[None, :]
    cos = jnp.cos(ang)
    sin = jnp.sin(ang)
    cos_t = jnp.tile(cos, (1, LANES // half))
    sin_t = jnp.tile(jnp.concatenate([-sin, sin], axis=1), (1, LANES // ATTN_HEAD_DIM))
    return cos_t, sin_t


def _row_tile(t, target):
    tm = target
    while t % tm:
        tm //= 2
    return tm


def kernel(x, meta_tokens, norm1_g, w_in, b_gate, conv_w, conv_b, dt_bias, a_log, d_skip, ssd_norm_g,
           q_norm_g, k_norm_g, sinks, w_ssd_down, w_attn_down, w_o, norm2_g, w_mlp_up, w_mlp_down):
    bsz, seq, _ = x.shape
    depth = w_in.shape[0]
    assert seq % BLOCK == 0
    lp = seq + BLOCK
    t = bsz * lp

    meta = jnp.broadcast_to(meta_tokens[None].astype(x.dtype), (bsz, N_META, D_MODEL))
    h = jnp.concatenate([jnp.zeros((bsz, PAD, D_MODEL), x.dtype), meta, x], axis=1).reshape(t, D_MODEL)

    cos_t, sin_t = _rope_tables(lp)
    expand = (jnp.arange(LANES)[:, None] == (jnp.arange(D_INNER)[None, :] // SSD_HEAD_DIM)).astype(BF16)
    bd = (((jnp.arange(LANES)[:, None] // ATTN_HEAD_DIM)
           == (jnp.arange(LANES)[None, :] // ATTN_HEAD_DIM)) * (1.0 / ATTN_HEAD_DIM)).astype(BF16)

    ri = jnp.arange((CONV_WIDTH - 1) * BLOCK)
    src_row = BLOCK + ri % BLOCK - (ri // BLOCK + 1)
    shift_any = (jnp.arange(2 * BLOCK)[None, :] == src_row[:, None])
    shift_first = jnp.logical_and(shift_any, jnp.arange(2 * BLOCK)[None, :] >= BLOCK + PAD)
    shift = jnp.stack([shift_first, shift_any]).astype(BF16)

    o_z, o_xbc, o_dt = 0, D_INNER, D_INNER + CONV_DIM
    o_q = o_dt + SSD_HEADS
    o_k, o_v, o_g = o_q + D_ATTN, o_q + D_ATTN + D_KV, o_q + D_ATTN + 2 * D_KV

    tm_proj = _row_tile(t, 1024)
    tm_rows = _row_tile(t, 512)

    for l in range(depth):
        w = w_in[l]
        w_main = jnp.concatenate(
            [w[:, o_xbc:o_dt], w[:, o_z:o_xbc], w[:, o_g:], w[:, o_q:o_k], w[:, o_k:o_v], w[:, o_v:o_g]],
            axis=1).astype(BF16)
        w_dt = jnp.pad(w[:, o_dt:o_q], ((0, 0), (0, LANES - SSD_HEADS))).astype(BF16)
        proj, dt_raw = _inproj(h, norm1_g[l][None], w_main, w_dt, tm=tm_proj, tn=N_PROJ // 4)
        proj3 = proj.reshape(bsz, lp, N_PROJ)

        pad_heads = lambda v: jnp.pad(v, (0, LANES - SSD_HEADS))[None]
        y_ssd = _ssd(proj3, dt_raw.reshape(bsz, lp, LANES), shift, conv_w[l], conv_b[l][None],
                     pad_heads(dt_bias[l]), pad_heads(a_log[l]),
                     jnp.repeat(d_skip[l], SSD_HEAD_DIM)[None], ssd_norm_g[l][None], expand)
        y_attn = _attn(proj3, sinks[l], cos_t, sin_t,
                       jnp.tile(q_norm_g[l], LANES // ATTN_HEAD_DIM)[None],
                       jnp.tile(k_norm_g[l], LANES // ATTN_HEAD_DIM)[None], bd)

        h = _merge(h, y_ssd.reshape(t, D_INNER), y_attn.reshape(t, D_ATTN), proj, b_gate[l][None],
                   w_ssd_down[l].astype(BF16), w_attn_down[l].astype(BF16), w_o[l].astype(BF16), tm=tm_rows)
        h = _mlp(h, norm2_g[l][None], w_mlp_up[l].astype(BF16), w_mlp_down[l].astype(BF16),
                 tm=tm_rows, ff_chunk=1024)

    return h.reshape(bsz, lp, D_MODEL)[:, BLOCK:]
```

```python
import functools

import jax
import jax.numpy as jnp
from jax import lax
from jax.experimental import pallas as pl
from jax.experimental.pallas import tpu as pltpu

F32 = jnp.float32
BF16 = jnp.bfloat16

D_MODEL = 1024
N_META = 16
BLOCK = 128
EPS = 1e-6
D_INNER = 2048
SSD_HEAD_DIM = 64
SSD_HEADS = 32
SSD_GROUPS = 8
HEADS_PER_GROUP = 4
SSD_STATE = 128
CONV_WIDTH = 4
CONV_DIM = 4096
GROUP_WIDTH = D_INNER // SSD_GROUPS
ATTN_HEAD_DIM = 64
Q_HEADS = 16
KV_HEADS = 4
D_ATTN = 1024
D_KV = 256
ROPE_THETA = 10000.0
D_FF = 4096

PAD = BLOCK - N_META
LANES = 128

COL_XBC = 0
COL_Z = CONV_DIM
COL_GATE = COL_Z + D_INNER
COL_Q = COL_GATE + 2 * D_MODEL
COL_K = COL_Q + D_ATTN
COL_V = COL_K + D_KV
N_PROJ = COL_V + D_KV

PROJ_DTYPE = BF16
VMEM_LIMIT = 56 * 1024 * 1024
LOG2E = 1.4426950408889634


def _params(*sem):
    return pltpu.CompilerParams(dimension_semantics=sem, vmem_limit_bytes=VMEM_LIMIT)


def _sigmoid(x):
    return 1.0 / (1.0 + jnp.exp2(x * (-LOG2E)))


def _dot(a, b):
    return jnp.dot(a, b, preferred_element_type=F32)


def _dot_nt(a, b):
    return lax.dot_general(a, b, (((1,), (1,)), ((), ())), preferred_element_type=F32)


def _dot_hilo(a, b):
    hi = a.astype(BF16)
    lo = (a - hi.astype(F32)).astype(BF16)
    return _dot(hi, b) + _dot(lo, b)


def _inproj_kernel(h_ref, g_ref, w_ref, wdt_ref, proj_ref, dt_ref, xn_ref):
    @pl.when(pl.program_id(1) == 0)
    def _():
        x = h_ref[...]
        ms = jnp.mean(x * x, axis=-1, keepdims=True)
        xn = (x * lax.rsqrt(ms + EPS) * g_ref[...]).astype(BF16)
        xn_ref[...] = xn
        dt_ref[...] = _dot(xn, wdt_ref[...])

    proj_ref[...] = _dot(xn_ref[...], w_ref[...]).astype(proj_ref.dtype)


def _inproj(h, g, w, wdt, *, tm, tn):
    t = h.shape[0]
    return pl.pallas_call(
        _inproj_kernel,
        grid=(t // tm, N_PROJ // tn),
        in_specs=[
            pl.BlockSpec((tm, D_MODEL), lambda i, j: (i, 0)),
            pl.BlockSpec((1, D_MODEL), lambda i, j: (0, 0)),
            pl.BlockSpec((D_MODEL, tn), lambda i, j: (0, j)),
            pl.BlockSpec((D_MODEL, LANES), lambda i, j: (0, 0)),
        ],
        out_specs=[
            pl.BlockSpec((tm, tn), lambda i, j: (i, j)),
            pl.BlockSpec((tm, LANES), lambda i, j: (i, 0)),
        ],
        out_shape=[
            jax.ShapeDtypeStruct((t, N_PROJ), PROJ_DTYPE),
            jax.ShapeDtypeStruct((t, LANES), F32),
        ],
        scratch_shapes=[pltpu.VMEM((tm, D_MODEL), BF16)],
        compiler_params=_params("arbitrary", "arbitrary"),
        name="in_proj",
    )(h, g, w, wdt)


def _cumsum_rows(x):
    row = lax.broadcasted_iota(jnp.int32, x.shape, 0)
    shift = 1
    while shift < x.shape[0]:
        x = x + jnp.where(row >= shift, pltpu.roll(x, shift, axis=0), 0.0)
        shift *= 2
    return x


def _ssd_phases(xbc_ref, xprev_ref, z_ref, dt_ref, shift_ref, cw_ref, cb_ref, dtb_ref, alog_ref, dskip_ref, ng_ref,
                expand_ref, y_ref, state_ref):
    n = pl.program_id(1)
    row = lax.broadcasted_iota(jnp.int32, (BLOCK, 1), 0)
    valid = jnp.logical_or(n > 0, row >= PAD)

    @pl.when(n == 0)
    def _():
        state_ref[...] = jnp.zeros_like(state_ref)

    x_cur = xbc_ref[0]
    shifted = _dot(shift_ref[0], jnp.concatenate([xprev_ref[0], x_cur], axis=0))
    acc = cb_ref[...] + cw_ref[CONV_WIDTH - 1:CONV_WIDTH, :] * x_cur.astype(F32)
    for s in range(1, CONV_WIDTH):
        acc = acc + cw_ref[CONV_WIDTH - 1 - s:CONV_WIDTH - s, :] * shifted[(s - 1) * BLOCK:s * BLOCK]
    yield
    xc = acc * _sigmoid(acc)
    xs = xc[:, :D_INNER]
    yield

    lane = lax.broadcasted_iota(jnp.int32, (BLOCK, LANES), 1)
    dtr = dt_ref[0] + dtb_ref[...]
    dt = jnp.maximum(dtr, 0.0) + jnp.log1p(jnp.exp(-jnp.abs(dtr)))
    dt = jnp.where(jnp.logical_and(valid, lane < SSD_HEADS), dt, 0.0)
    a_neg = jnp.where(lane[0:1] < SSD_HEADS, -LOG2E * jnp.exp(alog_ref[...]), 0.0)
    acum = _cumsum_rows(dt * a_neg)
    atot = acum[BLOCK - 1:BLOCK, :]
    acum_t = acum.T

    per_head = jnp.concatenate([dt, jnp.exp2(acum), jnp.exp2(atot - acum)], axis=0)
    wide = _dot_hilo(per_head, expand_ref[...])
    dt_w = wide[0:BLOCK]
    ea_w = wide[BLOCK:2 * BLOCK]
    dec_w = wide[2 * BLOCK:3 * BLOCK]
    xdt = xs * dt_w
    xdt_b = xdt.astype(BF16)
    xw_b = (xdt * dec_w).astype(BF16)
    chunk_decay = ea_w[BLOCK - 1:BLOCK, :]
    yield

    li = lax.broadcasted_iota(jnp.int32, (BLOCK, BLOCK), 0)
    si = lax.broadcasted_iota(jnp.int32, (BLOCK, BLOCK), 1)
    tril = li >= si
    lane_lo = lane < SSD_HEAD_DIM

    y_parts = []
    for g in range(SSD_GROUPS):
        c0 = g * GROUP_WIDTH
        bg = xc[:, D_INNER + g * SSD_STATE:D_INNER + (g + 1) * SSD_STATE]
        cg = xc[:, D_INNER + SSD_GROUPS * SSD_STATE + g * SSD_STATE:
                D_INNER + SSD_GROUPS * SSD_STATE + (g + 1) * SSD_STATE].astype(BF16)
        bg_t = bg.T.astype(BF16)
        cb = _dot(cg, bg_t)
        st_old = state_ref[:, c0:c0 + GROUP_WIDTH]
        y_off = _dot(cg, st_old.astype(BF16)) * ea_w[:, c0:c0 + GROUP_WIDTH]
        st_new = _dot(bg_t, xw_b[:, c0:c0 + GROUP_WIDTH])
        state_ref[:, c0:c0 + GROUP_WIDTH] = st_old * chunk_decay[:, c0:c0 + GROUP_WIDTH] + st_new
        for pair in range(HEADS_PER_GROUP // 2):
            ms = []
            for r in range(2):
                hd = g * HEADS_PER_GROUP + 2 * pair + r
                diff = acum[:, hd:hd + 1] - acum_t[hd:hd + 1, :]
                lm = jnp.exp2(jnp.where(tril, diff, -jnp.inf))
                ms.append((cb * lm).astype(BF16))
            lhs = jnp.concatenate(ms, axis=1)
            xp = xdt_b[:, c0 + pair * LANES:c0 + (pair + 1) * LANES]
            zero = jnp.zeros_like(xp)
            rhs = jnp.concatenate([jnp.where(lane_lo, xp, zero), jnp.where(lane_lo, zero, xp)], axis=0)
            y_diag = _dot(lhs, rhs)
            y_parts.append(y_diag + y_off[:, pair * LANES:(pair + 1) * LANES])
        if g % 2 == 1:
            yield
    y = jnp.concatenate(y_parts, axis=1) + xs * dskip_ref[...]
    zf = z_ref[0].astype(F32)
    y = y * (zf * _sigmoid(zf))
    yield
    outs = []
    for g in range(SSD_GROUPS):
        yg = y[:, g * GROUP_WIDTH:(g + 1) * GROUP_WIDTH]
        ms = jnp.mean(yg * yg, axis=-1, keepdims=True)
        outs.append(yg * lax.rsqrt(ms + EPS))
    y_ref[0] = (jnp.concatenate(outs, axis=1) * ng_ref[...]).astype(y_ref.dtype)


def _attn_phases(sinks_ref, q_ref, k_ref, v_ref, cos_ref, sin_ref, qg_ref, kg_ref, bd_ref, o_ref,
                 kprev_ref, vprev_ref, kmeta_ref, vmeta_ref):
    n = pl.program_id(1)
    rep = Q_HEADS // KV_HEADS
    wide = rep * BLOCK
    lane = lax.broadcasted_iota(jnp.int32, (BLOCK, LANES), 1)
    row = lax.broadcasted_iota(jnp.int32, (BLOCK, LANES), 0)
    lane_lo = lane < ATTN_HEAD_DIM
    row_lo = row < ATTN_HEAD_DIM
    first_half = (lane % ATTN_HEAD_DIM) < (ATTN_HEAD_DIM // 2)
    cos = cos_ref[...]
    sin = sin_ref[...]
    bd = bd_ref[...]

    def norm_rope(t, gain):
        ms = _dot_hilo(t * t, bd)
        t = t * lax.rsqrt(ms + EPS) * gain
        partner = jnp.where(first_half, pltpu.roll(t, LANES - 32, axis=1), pltpu.roll(t, 32, axis=1))
        return t * cos + partner * sin

    kd_cur, vt_cur = [], []
    k_gain = kg_ref[...] * (ATTN_HEAD_DIM ** -0.5)
    for c in range(D_KV // LANES):
        csl = slice(c * LANES, (c + 1) * LANES)
        kr = norm_rope(k_ref[0, :, csl].astype(F32), k_gain)
        sw = pltpu.roll(kr, ATTN_HEAD_DIM, axis=1)
        kd_cur += [jnp.where(lane_lo, kr, sw).astype(BF16), jnp.where(lane_lo, sw, kr).astype(BF16)]
        vt = v_ref[0, :, csl].astype(F32).T
        lo, hi = vt[:ATTN_HEAD_DIM], vt[ATTN_HEAD_DIM:]
        vt_cur += [jnp.concatenate([lo, lo], axis=0).astype(BF16), jnp.concatenate([hi, hi], axis=0).astype(BF16)]
    yield

    @pl.when(n == 0)
    def _():
        kprev_ref[...] = jnp.zeros_like(kprev_ref)
        vprev_ref[...] = jnp.zeros_like(vprev_ref)
        for j in range(KV_HEADS):
            kmeta_ref[j] = kd_cur[j][PAD:, :]
            vmeta_ref[j] = vt_cur[j]

    neg = -jnp.inf
    key_i = lax.broadcasted_iota(jnp.int32, (BLOCK, wide), 0)
    qry_i = lax.broadcasted_iota(jnp.int32, (BLOCK, wide), 1) % BLOCK
    tri = key_i <= qry_i
    band_bias = jnp.where(tri, jnp.where(n >= 1, 0.0, neg), jnp.where(n >= 2, 0.0, neg))
    meta_key = lax.broadcasted_iota(jnp.int32, (N_META, wide), 0)
    meta_qry = lax.broadcasted_iota(jnp.int32, (N_META, wide), 1) % BLOCK
    meta_ok = jnp.logical_or(n >= 1, meta_key <= meta_qry - PAD)
    meta_bias = jnp.where(meta_ok, 0.0, neg)
    zeros_pad = jnp.zeros((PAD, wide), BF16)

    qs, sink = [], []
    for j in range(KV_HEADS):
        rows, sk = [], []
        for qc in range(2):
            col = (2 * j + qc) * LANES
            qr = norm_rope(q_ref[0, :, col:col + LANES].astype(F32), qg_ref[...])
            rows.append(jnp.where(lane_lo, qr, 0.0))
            rows.append(jnp.where(lane_lo, 0.0, qr))
            for r in range(2):
                sk.append(jnp.full((1, BLOCK), sinks_ref[rep * j + 2 * qc + r], F32))
        qs.append(jnp.concatenate(rows, axis=0).astype(BF16))
        yield
        sink.append(jnp.concatenate(sk, axis=1))
    s_cur = [_dot_nt(kd_cur[j], qs[j]) for j in range(KV_HEADS)]
    s_prev = [_dot_nt(kprev_ref[j], qs[j]) for j in range(KV_HEADS)]
    s_meta = [_dot_nt(kmeta_ref[j], qs[j]) + meta_bias for j in range(KV_HEADS)]
    s_band = [jnp.where(tri, s_cur[j], s_prev[j]) + band_bias for j in range(KV_HEADS)]
    yield
    m = [jnp.maximum(jnp.maximum(jnp.max(s_band[j], axis=0, keepdims=True),
                                 jnp.max(s_meta[j], axis=0, keepdims=True)), sink[j]) for j in range(KV_HEADS)]
    p_band = [jnp.exp(s_band[j] - m[j]) for j in range(KV_HEADS)]
    p_meta = [jnp.exp(s_meta[j] - m[j]) for j in range(KV_HEADS)]
    denom = [jnp.sum(p_band[j], axis=0, keepdims=True) + jnp.sum(p_meta[j], axis=0, keepdims=True)
             + jnp.exp(sink[j] - m[j]) for j in range(KV_HEADS)]
    yield
    p_cur = [jnp.where(tri, p_band[j], 0.0) for j in range(KV_HEADS)]
    p_all = [jnp.concatenate([p_cur[j].astype(BF16), (p_band[j] - p_cur[j]).astype(BF16), zeros_pad,
                              p_meta[j].astype(BF16)], axis=0) for j in range(KV_HEADS)]
    yield
    o_t = [_dot(jnp.concatenate([vt_cur[j], vprev_ref[j], vmeta_ref[j]], axis=1), p_all[j]) * (1.0 / denom[j])
           for j in range(KV_HEADS)]
    yield
    out_cols = []
    for j in range(KV_HEADS):
        for qc in range(2):
            pair = jnp.where(row_lo, o_t[j][:, (2 * qc) * BLOCK:(2 * qc + 1) * BLOCK],
                             o_t[j][:, (2 * qc + 1) * BLOCK:(2 * qc + 2) * BLOCK])
            out_cols.append(pair.T)
    o_ref[0] = jnp.concatenate(out_cols, axis=1).astype(o_ref.dtype)
    for j in range(KV_HEADS):
        kprev_ref[j] = kd_cur[j]
        vprev_ref[j] = vt_cur[j]


def _mixer_kernel(xbc_ref, xprev_ref, z_ref, dt_ref, shift_ref, cw_ref, cb_ref, dtb_ref, alog_ref, dskip_ref, ng_ref,
                  expand_ref, sinks_ref, q_ref, k_ref, v_ref, cos_ref, sin_ref, qg_ref, kg_ref, bd_ref,
                  y_ref, o_ref, state_ref, kprev_ref, vprev_ref, kmeta_ref, vmeta_ref):
    ssd = _ssd_phases(xbc_ref, xprev_ref, z_ref, dt_ref, shift_ref, cw_ref, cb_ref, dtb_ref, alog_ref, dskip_ref,
                      ng_ref, expand_ref, y_ref, state_ref)
    att = _attn_phases(sinks_ref, q_ref, k_ref, v_ref, cos_ref, sin_ref, qg_ref, kg_ref, bd_ref, o_ref,
                       kprev_ref, vprev_ref, kmeta_ref, vmeta_ref)
    live = [att, ssd]
    while live:
        for gen in list(live):
            if next(gen, "done") == "done":
                live.remove(gen)


def _mixer(proj, dt_raw, shift, cw, cb, dtb, alog, dskip, ng, expand, sinks, cos_t, sin_t, qg, kg, bd):
    bsz, lp, _ = proj.shape
    const = lambda shape: pl.BlockSpec(shape, lambda b, n: (0, 0))
    return pl.pallas_call(
        _mixer_kernel,
        grid=(bsz, lp // BLOCK),
        in_specs=[
            pl.BlockSpec((1, BLOCK, CONV_DIM), lambda b, n: (b, n, COL_XBC // CONV_DIM)),
            pl.BlockSpec((1, BLOCK, CONV_DIM), lambda b, n: (b, jnp.maximum(n - 1, 0), COL_XBC // CONV_DIM)),
            pl.BlockSpec((1, BLOCK, D_INNER), lambda b, n: (b, n, COL_Z // D_INNER)),
            pl.BlockSpec((1, BLOCK, LANES), lambda b, n: (b, n, 0)),
            pl.BlockSpec((1, (CONV_WIDTH - 1) * BLOCK, 2 * BLOCK), lambda b, n: (jnp.minimum(n, 1), 0, 0)),
            const((CONV_WIDTH, CONV_DIM)),
            const((1, CONV_DIM)),
            const((1, LANES)),
            const((1, LANES)),
            const((1, D_INNER)),
            const((1, D_INNER)),
            const((LANES, D_INNER)),
            pl.BlockSpec(memory_space=pltpu.SMEM),
            pl.BlockSpec((1, BLOCK, D_ATTN), lambda b, n: (b, n, COL_Q // D_ATTN)),
            pl.BlockSpec((1, BLOCK, D_KV), lambda b, n: (b, n, COL_K // D_KV)),
            pl.BlockSpec((1, BLOCK, D_KV), lambda b, n: (b, n, COL_V // D_KV)),
            pl.BlockSpec((BLOCK, LANES), lambda b, n: (n, 0)),
            pl.BlockSpec((BLOCK, LANES), lambda b, n: (n, 0)),
            const((1, LANES)),
            const((1, LANES)),
            const((LANES, LANES)),
        ],
        out_specs=[pl.BlockSpec((1, BLOCK, D_INNER), lambda b, n: (b, n, 0)),
                   pl.BlockSpec((1, BLOCK, D_ATTN), lambda b, n: (b, n, 0))],
        out_shape=[jax.ShapeDtypeStruct((bsz, lp, D_INNER), BF16), jax.ShapeDtypeStruct((bsz, lp, D_ATTN), BF16)],
        scratch_shapes=[
            pltpu.VMEM((SSD_STATE, D_INNER), F32),
            pltpu.VMEM((KV_HEADS, BLOCK, LANES), BF16),
            pltpu.VMEM((KV_HEADS, LANES, BLOCK), BF16),
            pltpu.VMEM((KV_HEADS, N_META, LANES), BF16),
            pltpu.VMEM((KV_HEADS, LANES, BLOCK), BF16),
        ],
        compiler_params=_params("arbitrary", "arbitrary"),
        name="token_mixers",
    )(proj, proj, proj, dt_raw, shift, cw, cb, dtb, alog, dskip, ng, expand, sinks, proj, proj, proj, cos_t, sin_t, qg, kg, bd)


def _merge_kernel(h_ref, ys_ref, ya_ref, gl_ref, bg_ref, wsd_ref, wad_ref, wo_ref, o_ref):
    gl = gl_ref[...].astype(F32) + bg_ref[...]
    g_ssd = _sigmoid(gl[:, :D_MODEL])
    g_attn = _sigmoid(gl[:, D_MODEL:])
    merged = g_ssd * _dot(ys_ref[...], wsd_ref[...]) + g_attn * _dot(ya_ref[...], wad_ref[...])
    o_ref[...] = h_ref[...] + _dot(merged.astype(BF16), wo_ref[...])


def _merge(h, ys, ya, proj, bg, wsd, wad, wo, *, tm):
    t = h.shape[0]
    row = lambda w: pl.BlockSpec((tm, w), lambda i: (i, 0))
    const = lambda shape: pl.BlockSpec(shape, lambda i: (0, 0))
    return pl.pallas_call(
        _merge_kernel,
        grid=(t // tm,),
        in_specs=[
            row(D_MODEL), row(D_INNER), row(D_ATTN),
            pl.BlockSpec((tm, 2 * D_MODEL), lambda i: (i, COL_GATE // (2 * D_MODEL))),
            const((1, 2 * D_MODEL)),
            const((D_INNER, D_MODEL)), const((D_ATTN, D_MODEL)), const((D_MODEL, D_MODEL)),
        ],
        out_specs=row(D_MODEL),
        out_shape=jax.ShapeDtypeStruct((t, D_MODEL), F32),
        input_output_aliases={0: 0},
        compiler_params=_params("arbitrary"),
        name="merge_out_proj",
    )(h, ys, ya, proj, bg, wsd, wad, wo)


def _mlp_kernel(h_ref, g_ref, wup_ref, wdn_ref, o_ref, *, ff_chunk):
    x = h_ref[...]
    ms = jnp.mean(x * x, axis=-1, keepdims=True)
    xn = (x * lax.rsqrt(ms + EPS) * g_ref[...]).astype(BF16)
    acc = x
    for c in range(D_FF // ff_chunk):
        u = jnp.maximum(_dot(xn, wup_ref[:, c * ff_chunk:(c + 1) * ff_chunk]), 0.0)
        acc = acc + _dot((u * u).astype(BF16), wdn_ref[c * ff_chunk:(c + 1) * ff_chunk, :])
    o_ref[...] = acc


def _mlp(h, g, wup, wdn, *, tm, ff_chunk):
    t = h.shape[0]
    const = lambda shape: pl.BlockSpec(shape, lambda i: (0, 0))
    return pl.pallas_call(
        functools.partial(_mlp_kernel, ff_chunk=ff_chunk),
        grid=(t // tm,),
        in_specs=[
            pl.BlockSpec((tm, D_MODEL), lambda i: (i, 0)),
            const((1, D_MODEL)),
            const((D_MODEL, D_FF)),
            const((D_FF, D_MODEL)),
        ],
        out_specs=pl.BlockSpec((tm, D_MODEL), lambda i: (i, 0)),
        out_shape=jax.ShapeDtypeStruct((t, D_MODEL), F32),
        input_output_aliases={0: 0},
        compiler_params=_params("arbitrary"),
        name="mlp",
    )(h, g, wup, wdn)


def _rope_tables(lp):
    half = ATTN_HEAD_DIM // 2
    inv_freq = ROPE_THETA ** (-jnp.arange(half, dtype=F32) / half)
    pos = (jnp.arange(lp) - PAD).astype(F32)
    ang = pos[:, None] * inv_freq[None, :]
    cos = jnp.cos(ang)
    sin = jnp.sin(ang)
    cos_t = jnp.tile(cos, (1, LANES // half))
    sin_t = jnp.tile(jnp.concatenate([-sin, sin], axis=1), (1, LANES // ATTN_HEAD_DIM))
    return cos_t, sin_t


def _row_tile(t, target):
    tm = target
    while t % tm:
        tm //= 2
    return tm


def kernel(x, meta_tokens, norm1_g, w_in, b_gate, conv_w, conv_b, dt_bias, a_log, d_skip, ssd_norm_g,
           q_norm_g, k_norm_g, sinks, w_ssd_down, w_attn_down, w_o, norm2_g, w_mlp_up, w_mlp_down):
    bsz, seq, _ = x.shape
    depth = w_in.shape[0]
    assert seq % BLOCK == 0
    lp = seq + BLOCK
    t = bsz * lp

    meta = jnp.broadcast_to(meta_tokens[None].astype(x.dtype), (bsz, N_META, D_MODEL))
    h = jnp.concatenate([jnp.zeros((bsz, PAD, D_MODEL), x.dtype), meta, x], axis=1).reshape(t, D_MODEL)

    cos_t, sin_t = _rope_tables(lp)
    expand = (jnp.arange(LANES)[:, None] == (jnp.arange(D_INNER)[None, :] // SSD_HEAD_DIM)).astype(BF16)
    bd = (((jnp.arange(LANES)[:, None] // ATTN_HEAD_DIM)
           == (jnp.arange(LANES)[None, :] // ATTN_HEAD_DIM)) * (1.0 / ATTN_HEAD_DIM)).astype(BF16)

    ri = jnp.arange((CONV_WIDTH - 1) * BLOCK)
    src_row = BLOCK + ri % BLOCK - (ri // BLOCK + 1)
    shift_any = (jnp.arange(2 * BLOCK)[None, :] == src_row[:, None])
    shift_first = jnp.logical_and(shift_any, jnp.arange(2 * BLOCK)[None, :] >= BLOCK + PAD)
    shift = jnp.stack([shift_first, shift_any]).astype(BF16)

    o_z, o_xbc, o_dt = 0, D_INNER, D_INNER + CONV_DIM
    o_q = o_dt + SSD_HEADS
    o_k, o_v, o_g = o_q + D_ATTN, o_q + D_ATTN + D_KV, o_q + D_ATTN + 2 * D_KV

    tm_proj = _row_tile(t, 1024)
    tm_rows = _row_tile(t, 512)

    for l in range(depth):
        w = w_in[l]
        w_main = jnp.concatenate(
            [w[:, o_xbc:o_dt], w[:, o_z:o_xbc], w[:, o_g:], w[:, o_q:o_k], w[:, o_k:o_v], w[:, o_v:o_g]],
            axis=1).astype(BF16)
        w_dt = jnp.pad(w[:, o_dt:o_q], ((0, 0), (0, LANES - SSD_HEADS))).astype(BF16)
        proj, dt_raw = _inproj(h, norm1_g[l][None], w_main, w_dt, tm=tm_proj, tn=N_PROJ // 2)
        proj3 = proj.reshape(bsz, lp, N_PROJ)

        pad_heads = lambda v: jnp.pad(v, (0, LANES - SSD_HEADS))[None]
        y_ssd, y_attn = _mixer(proj3, dt_raw.reshape(bsz, lp, LANES), shift, conv_w[l], conv_b[l][None],
                               pad_heads(dt_bias[l]), pad_heads(a_log[l]),
                               jnp.repeat(d_skip[l], SSD_HEAD_DIM)[None], ssd_norm_g[l][None], expand,
                               sinks[l], cos_t, sin_t,
                               jnp.tile(q_norm_g[l], LANES // ATTN_HEAD_DIM)[None],
                               jnp.tile(k_norm_g[l], LANES // ATTN_HEAD_DIM)[None], bd)

        h = _merge(h, y_ssd.reshape(t, D_INNER), y_attn.reshape(t, D_ATTN), proj, b_gate[l][None],
                   w_ssd_down[l].astype(BF16), w_attn_down[l].astype(BF16), w_o[l].astype(BF16), tm=tm_rows)
        h = _mlp(h, norm2_g[l][None], w_mlp_up[l].astype(BF16), w_mlp_down[l].astype(BF16),
                 tm=tm_rows, ff_chunk=1024)

    return h.reshape(bsz, lp, D_MODEL)[:, BLOCK:]
```

```python
import functools

import jax
import jax.numpy as jnp
from jax import lax
from jax.experimental import pallas as pl
from jax.experimental.pallas import tpu as pltpu

F32 = jnp.float32
BF16 = jnp.bfloat16

D_MODEL = 1024
N_META = 16
BLOCK = 128
EPS = 1e-6
D_INNER = 2048
SSD_HEAD_DIM = 64
SSD_HEADS = 32
SSD_GROUPS = 8
HEADS_PER_GROUP = 4
SSD_STATE = 128
CONV_WIDTH = 4
CONV_DIM = 4096
GROUP_WIDTH = D_INNER // SSD_GROUPS
ATTN_HEAD_DIM = 64
Q_HEADS = 16
KV_HEADS = 4
D_ATTN = 1024
D_KV = 256
ROPE_THETA = 10000.0
D_FF = 4096

PAD = BLOCK - N_META
LANES = 128

COL_XBC = 0
COL_Z = CONV_DIM
COL_GATE = COL_Z + D_INNER
COL_Q = COL_GATE + 2 * D_MODEL
COL_K = COL_Q + D_ATTN
COL_V = COL_K + D_KV
N_PROJ = COL_V + D_KV

PROJ_DTYPE = BF16
VMEM_LIMIT = 56 * 1024 * 1024
LOG2E = 1.4426950408889634
CONV_TAIL = 16


def _params(*sem):
    return pltpu.CompilerParams(dimension_semantics=sem, vmem_limit_bytes=VMEM_LIMIT)


def _sigmoid(x):
    return 1.0 / (1.0 + jnp.exp2(x * (-LOG2E)))


def _dot(a, b):
    return jnp.dot(a, b, preferred_element_type=F32)


def _dot_nt(a, b):
    return lax.dot_general(a, b, (((1,), (1,)), ((), ())), preferred_element_type=F32)


def _dot_hilo(a, b):
    hi = a.astype(BF16)
    lo = (a - hi.astype(F32)).astype(BF16)
    return _dot(hi, b) + _dot(lo, b)


def _inproj_kernel(h_ref, g_ref, w_ref, wdt_ref, proj_ref, dt_ref, xn_ref):
    @pl.when(pl.program_id(1) == 0)
    def _():
        x = h_ref[...]
        ms = jnp.mean(x * x, axis=-1, keepdims=True)
        xn = (x * lax.rsqrt(ms + EPS) * g_ref[...]).astype(BF16)
        xn_ref[...] = xn
        dt_ref[...] = _dot(xn, wdt_ref[...])

    proj_ref[...] = _dot(xn_ref[...], w_ref[...]).astype(proj_ref.dtype)


def _inproj(h, g, w, wdt, *, tm, tn):
    t = h.shape[0]
    return pl.pallas_call(
        _inproj_kernel,
        grid=(t // tm, N_PROJ // tn),
        in_specs=[
            pl.BlockSpec((tm, D_MODEL), lambda i, j: (i, 0)),
            pl.BlockSpec((1, D_MODEL), lambda i, j: (0, 0)),
            pl.BlockSpec((D_MODEL, tn), lambda i, j: (0, j)),
            pl.BlockSpec((D_MODEL, LANES), lambda i, j: (0, 0)),
        ],
        out_specs=[
            pl.BlockSpec((tm, tn), lambda i, j: (i, j)),
            pl.BlockSpec((tm, LANES), lambda i, j: (i, 0)),
        ],
        out_shape=[
            jax.ShapeDtypeStruct((t, N_PROJ), PROJ_DTYPE),
            jax.ShapeDtypeStruct((t, LANES), F32),
        ],
        scratch_shapes=[pltpu.VMEM((tm, D_MODEL), BF16)],
        compiler_params=_params("arbitrary", "arbitrary"),
        name="in_proj",
    )(h, g, w, wdt)


def _cumsum_rows(x):
    row = lax.broadcasted_iota(jnp.int32, x.shape, 0)
    shift = 1
    while shift < x.shape[0]:
        x = x + jnp.where(row >= shift, pltpu.roll(x, shift, axis=0), 0.0)
        shift *= 2
    return x


def _ssd_phases(prefix, xbc_ref, xprev_ref, xmeta_ref, state0_ref, z_ref, dt_ref, shift_ref, cw_ref, cb_ref, dtb_ref,
                alog_ref, dskip_ref, ng_ref, expand_ref, y_ref, state_out_ref, state_ref):
    n = pl.program_id(1)
    row = lax.broadcasted_iota(jnp.int32, (BLOCK, 1), 0)
    lane = lax.broadcasted_iota(jnp.int32, (BLOCK, LANES), 1)
    x_cur = xbc_ref[0]
    if prefix:
        live = jnp.logical_and(row >= PAD, lane < SSD_HEADS)
        state_ref[...] = jnp.zeros_like(state_ref)
        x_cat = jnp.concatenate([jnp.zeros_like(x_cur), x_cur], axis=0)
    else:
        live = lane < SSD_HEADS

        @pl.when(n == 0)
        def _():
            state_ref[...] = state0_ref[...]

        tail = jnp.where(n == 0, xmeta_ref[0, BLOCK - CONV_TAIL:, :], xprev_ref[0, BLOCK - CONV_TAIL:, :])
        x_cat = jnp.concatenate([jnp.zeros((BLOCK - CONV_TAIL, CONV_DIM), BF16), tail, x_cur], axis=0)

    shifted = _dot(shift_ref[...], x_cat)
    acc = cb_ref[...] + cw_ref[CONV_WIDTH - 1:CONV_WIDTH, :] * x_cur.astype(F32)
    for s in range(1, CONV_WIDTH):
        acc = acc + cw_ref[CONV_WIDTH - 1 - s:CONV_WIDTH - s, :] * shifted[(s - 1) * BLOCK:s * BLOCK]
    yield
    xc = acc * _sigmoid(acc)
    xs = xc[:, :D_INNER]
    yield

    dtr = dt_ref[0] + dtb_ref[...]
    dt = jnp.maximum(dtr, 0.0) + jnp.log1p(jnp.exp(-jnp.abs(dtr)))
    dt = jnp.where(live, dt, 0.0)
    a_neg = jnp.where(lane[0:1] < SSD_HEADS, -LOG2E * jnp.exp(alog_ref[...]), 0.0)
    acum = _cumsum_rows(dt * a_neg)
    atot = acum[BLOCK - 1:BLOCK, :]
    acum_t = acum.T

    per_head = jnp.concatenate([dt, jnp.exp2(acum), jnp.exp2(atot - acum)], axis=0)
    wide = _dot_hilo(per_head, expand_ref[...])
    dt_w = wide[0:BLOCK]
    ea_w = wide[BLOCK:2 * BLOCK]
    dec_w = wide[2 * BLOCK:3 * BLOCK]
    xdt = xs * dt_w
    xdt_b = xdt.astype(BF16)
    xw_b = (xdt * dec_w).astype(BF16)
    chunk_decay = ea_w[BLOCK - 1:BLOCK, :]
    yield

    li = lax.broadcasted_iota(jnp.int32, (BLOCK, BLOCK), 0)
    si = lax.broadcasted_iota(jnp.int32, (BLOCK, BLOCK), 1)
    tril = li >= si
    lane_lo = lane < SSD_HEAD_DIM

    y_parts = []
    for g in range(SSD_GROUPS):
        c0 = g * GROUP_WIDTH
        bg = xc[:, D_INNER + g * SSD_STATE:D_INNER + (g + 1) * SSD_STATE]
        cg = xc[:, D_INNER + SSD_GROUPS * SSD_STATE + g * SSD_STATE:
                D_INNER + SSD_GROUPS * SSD_STATE + (g + 1) * SSD_STATE].astype(BF16)
        bg_t = bg.T.astype(BF16)
        cb = _dot(cg, bg_t)
        st_old = state_ref[:, c0:c0 + GROUP_WIDTH]
        y_off = _dot(cg, st_old.astype(BF16)) * ea_w[:, c0:c0 + GROUP_WIDTH]
        st_new = _dot(bg_t, xw_b[:, c0:c0 + GROUP_WIDTH])
        state_ref[:, c0:c0 + GROUP_WIDTH] = st_old * chunk_decay[:, c0:c0 + GROUP_WIDTH] + st_new
        for pair in range(HEADS_PER_GROUP // 2):
            ms = []
            for r in range(2):
                hd = g * HEADS_PER_GROUP + 2 * pair + r
                diff = acum[:, hd:hd + 1] - acum_t[hd:hd + 1, :]
                lm = jnp.exp2(jnp.where(tril, diff, -jnp.inf))
                ms.append((cb * lm).astype(BF16))
            lhs = jnp.concatenate(ms, axis=1)
            xp = xdt_b[:, c0 + pair * LANES:c0 + (pair + 1) * LANES]
            zero = jnp.zeros_like(xp)
            rhs = jnp.concatenate([jnp.where(lane_lo, xp, zero), jnp.where(lane_lo, zero, xp)], axis=0)
            y_diag = _dot(lhs, rhs)
            y_parts.append(y_diag + y_off[:, pair * LANES:(pair + 1) * LANES])
        if g % 2 == 1:
            yield
    y = jnp.concatenate(y_parts, axis=1) + xs * dskip_ref[...]
    zf = z_ref[0].astype(F32)
    y = y * (zf * _sigmoid(zf))
    yield
    outs = []
    for g in range(SSD_GROUPS):
        yg = y[:, g * GROUP_WIDTH:(g + 1) * GROUP_WIDTH]
        ms = jnp.mean(yg * yg, axis=-1, keepdims=True)
        outs.append(yg * lax.rsqrt(ms + EPS))
    y_ref[0] = (jnp.concatenate(outs, axis=1) * ng_ref[...]).astype(y_ref.dtype)
    if prefix:
        state_out_ref[...] = state_ref[...]


def _attn_phases(prefix, sinks_ref, q_ref, k_ref, v_ref, cos_ref, sin_ref, qg_ref, kg_ref, bd_ref, kmeta_ref, vmeta_ref,
                 o_ref, kprev_ref, vprev_ref):
    n = pl.program_id(1) + (0 if prefix else 1)
    rep = Q_HEADS // KV_HEADS
    wide = rep * BLOCK
    lane = lax.broadcasted_iota(jnp.int32, (BLOCK, LANES), 1)
    row = lax.broadcasted_iota(jnp.int32, (BLOCK, LANES), 0)
    lane_lo = lane < ATTN_HEAD_DIM
    row_lo = row < ATTN_HEAD_DIM
    first_half = (lane % ATTN_HEAD_DIM) < (ATTN_HEAD_DIM // 2)
    cos = cos_ref[...]
    sin = sin_ref[...]
    bd = bd_ref[...]

    def norm_rope(t, gain):
        ms = _dot_hilo(t * t, bd)
        t = t * lax.rsqrt(ms + EPS) * gain
        partner = jnp.where(first_half, pltpu.roll(t, LANES - 32, axis=1), pltpu.roll(t, 32, axis=1))
        return t * cos + partner * sin

    kd_cur, vt_cur = [], []
    k_gain = kg_ref[...] * (ATTN_HEAD_DIM ** -0.5)
    for c in range(D_KV // LANES):
        csl = slice(c * LANES, (c + 1) * LANES)
        kr = norm_rope(k_ref[0, :, csl].astype(F32), k_gain)
        sw = pltpu.roll(kr, ATTN_HEAD_DIM, axis=1)
        kd_cur += [jnp.where(lane_lo, kr, sw).astype(BF16), jnp.where(lane_lo, sw, kr).astype(BF16)]
        vt = v_ref[0, :, csl].astype(F32).T
        lo, hi = vt[:ATTN_HEAD_DIM], vt[ATTN_HEAD_DIM:]
        vt_cur += [jnp.concatenate([lo, lo], axis=0).astype(BF16), jnp.concatenate([hi, hi], axis=0).astype(BF16)]
    yield

    @pl.when(n <= 1)
    def _():
        kprev_ref[...] = jnp.zeros_like(kprev_ref)
        vprev_ref[...] = jnp.zeros_like(vprev_ref)

    if prefix:
        for j in range(KV_HEADS):
            kmeta_ref[j] = kd_cur[j][PAD:, :]
            vmeta_ref[j] = vt_cur[j]

    neg = -jnp.inf
    key_i = lax.broadcasted_iota(jnp.int32, (BLOCK, wide), 0)
    qry_i = lax.broadcasted_iota(jnp.int32, (BLOCK, wide), 1) % BLOCK
    tri = key_i <= qry_i
    band_bias = jnp.where(tri, jnp.where(n >= 1, 0.0, neg), jnp.where(n >= 2, 0.0, neg))
    meta_key = lax.broadcasted_iota(jnp.int32, (N_META, wide), 0)
    meta_qry = lax.broadcasted_iota(jnp.int32, (N_META, wide), 1) % BLOCK
    meta_ok = jnp.logical_or(n >= 1, meta_key <= meta_qry - PAD)
    meta_bias = jnp.where(meta_ok, 0.0, neg)
    zeros_pad = jnp.zeros((PAD, wide), BF16)

    qs, sink = [], []
    for j in range(KV_HEADS):
        rows, sk = [], []
        for qc in range(2):
            col = (2 * j + qc) * LANES
            qr = norm_rope(q_ref[0, :, col:col + LANES].astype(F32), qg_ref[...])
            rows.append(jnp.where(lane_lo, qr, 0.0))
            rows.append(jnp.where(lane_lo, 0.0, qr))
            for r in range(2):
                sk.append(jnp.full((1, BLOCK), sinks_ref[rep * j + 2 * qc + r], F32))
        qs.append(jnp.concatenate(rows, axis=0).astype(BF16))
        yield
        sink.append(jnp.concatenate(sk, axis=1))
    s_cur = [_dot_nt(kd_cur[j], qs[j]) for j in range(KV_HEADS)]
    s_prev = [_dot_nt(kprev_ref[j], qs[j]) for j in range(KV_HEADS)]
    s_meta = [_dot_nt(kmeta_ref[j], qs[j]) + meta_bias for j in range(KV_HEADS)]
    s_band = [jnp.where(tri, s_cur[j], s_prev[j]) + band_bias for j in range(KV_HEADS)]
    yield
    m = [jnp.maximum(jnp.maximum(jnp.max(s_band[j], axis=0, keepdims=True),
                                 jnp.max(s_meta[j], axis=0, keepdims=True)), sink[j]) for j in range(KV_HEADS)]
    p_band = [jnp.exp(s_band[j] - m[j]) for j in range(KV_HEADS)]
    p_meta = [jnp.exp(s_meta[j] - m[j]) for j in range(KV_HEADS)]
    denom = [jnp.sum(p_band[j], axis=0, keepdims=True) + jnp.sum(p_meta[j], axis=0, keepdims=True)
             + jnp.exp(sink[j] - m[j]) for j in range(KV_HEADS)]
    yield
    p_cur = [jnp.where(tri, p_band[j], 0.0) for j in range(KV_HEADS)]
    p_all = [jnp.concatenate([p_cur[j].astype(BF16), (p_band[j] - p_cur[j]).astype(BF16), zeros_pad,
                              p_meta[j].astype(BF16)], axis=0) for j in range(KV_HEADS)]
    yield
    o_t = [_dot(jnp.concatenate([vt_cur[j], vprev_ref[j], vmeta_ref[j]], axis=1), p_all[j]) * (1.0 / denom[j])
           for j in range(KV_HEADS)]
    yield
    out_cols = []
    for j in range(KV_HEADS):
        for qc in range(2):
            pair = jnp.where(row_lo, o_t[j][:, (2 * qc) * BLOCK:(2 * qc + 1) * BLOCK],
                             o_t[j][:, (2 * qc + 1) * BLOCK:(2 * qc + 2) * BLOCK])
            out_cols.append(pair.T)
    o_ref[0] = jnp.concatenate(out_cols, axis=1).astype(o_ref.dtype)
    for j in range(KV_HEADS):
        kprev_ref[j] = kd_cur[j]
        vprev_ref[j] = vt_cur[j]


def _mixer_kernel(*refs, prefix):
    if prefix:
        (xbc_ref, z_ref, dt_ref, shift_ref, cw_ref, cb_ref, dtb_ref, alog_ref, dskip_ref, ng_ref, expand_ref,
         sinks_ref, q_ref, k_ref, v_ref, cos_ref, sin_ref, qg_ref, kg_ref, bd_ref,
         y_ref, o_ref, state_out_ref, kmeta_ref, vmeta_ref, state_ref, kprev_ref, vprev_ref) = refs
        xprev_ref = xmeta_ref = state0_ref = None
    else:
        (xbc_ref, xprev_ref, xmeta_ref, state0_ref, z_ref, dt_ref, shift_ref, cw_ref, cb_ref, dtb_ref, alog_ref,
         dskip_ref, ng_ref, expand_ref, sinks_ref, q_ref, k_ref, v_ref, cos_ref, sin_ref, qg_ref, kg_ref, bd_ref,
         kmeta_ref, vmeta_ref, y_ref, o_ref, state_ref, kprev_ref, vprev_ref) = refs
        state_out_ref = None
    ssd = _ssd_phases(prefix, xbc_ref, xprev_ref, xmeta_ref, state0_ref, z_ref, dt_ref, shift_ref, cw_ref, cb_ref,
                      dtb_ref, alog_ref, dskip_ref, ng_ref, expand_ref, y_ref, state_out_ref, state_ref)
    att = _attn_phases(prefix, sinks_ref, q_ref, k_ref, v_ref, cos_ref, sin_ref, qg_ref, kg_ref, bd_ref,
                       kmeta_ref, vmeta_ref, o_ref, kprev_ref, vprev_ref)
    live = [att, ssd]
    while live:
        for gen in list(live):
            if next(gen, "done") == "done":
                live.remove(gen)


def _mixer(proj, dt_raw, shift, cw, cb, dtb, alog, dskip, ng, expand, sinks, cos_t, sin_t, qg, kg, bd, carry=None):
    prefix = carry is None
    bsz, lp, _ = proj.shape
    const = lambda shape: pl.BlockSpec(shape, lambda b, n: (0,) * len(shape))
    blk = lambda width, col: pl.BlockSpec((1, BLOCK, width), lambda b, n: (b, n, col // width))
    kmeta_shape, vmeta_shape = (KV_HEADS, N_META, LANES), (KV_HEADS, LANES, BLOCK)
    ssd_in = [blk(CONV_DIM, COL_XBC)]
    ssd_args = [proj]
    if not prefix:
        xmeta, state0, kmeta, vmeta = carry
        ssd_in += [pl.BlockSpec((1, BLOCK, CONV_DIM), lambda b, n: (b, jnp.maximum(n - 1, 0), COL_XBC // CONV_DIM)),
                   pl.BlockSpec((1, BLOCK, CONV_DIM), lambda b, n: (0, 0, COL_XBC // CONV_DIM)),
                   const((SSD_STATE, D_INNER))]
        ssd_args += [proj, xmeta, state0]
    ssd_in += [blk(D_INNER, COL_Z), blk(LANES, 0), const(((CONV_WIDTH - 1) * BLOCK, 2 * BLOCK)),
               const((CONV_WIDTH, CONV_DIM)), const((1, CONV_DIM)), const((1, LANES)), const((1, LANES)),
               const((1, D_INNER)), const((1, D_INNER)), const((LANES, D_INNER))]
    ssd_args += [proj, dt_raw, shift, cw, cb, dtb, alog, dskip, ng, expand]
    att_in = [pl.BlockSpec(memory_space=pltpu.SMEM), blk(D_ATTN, COL_Q), blk(D_KV, COL_K), blk(D_KV, COL_V),
              pl.BlockSpec((BLOCK, LANES), lambda b, n: (n, 0)), pl.BlockSpec((BLOCK, LANES), lambda b, n: (n, 0)),
              const((1, LANES)), const((1, LANES)), const((LANES, LANES))]
    att_args = [sinks, proj, proj, proj, cos_t, sin_t, qg, kg, bd]
    out_specs = [pl.BlockSpec((1, BLOCK, D_INNER), lambda b, n: (b, n, 0)),
                 pl.BlockSpec((1, BLOCK, D_ATTN), lambda b, n: (b, n, 0))]
    out_shape = [jax.ShapeDtypeStruct((bsz, lp, D_INNER), BF16), jax.ShapeDtypeStruct((bsz, lp, D_ATTN), BF16)]
    if prefix:
        out_specs += [const((SSD_STATE, D_INNER)), const(kmeta_shape), const(vmeta_shape)]
        out_shape += [jax.ShapeDtypeStruct((SSD_STATE, D_INNER), F32), jax.ShapeDtypeStruct(kmeta_shape, BF16),
                      jax.ShapeDtypeStruct(vmeta_shape, BF16)]
    else:
        att_in += [const(kmeta_shape), const(vmeta_shape)]
        att_args += [kmeta, vmeta]
    return pl.pallas_call(
        functools.partial(_mixer_kernel, prefix=prefix),
        grid=(bsz, lp // BLOCK),
        in_specs=ssd_in + att_in,
        out_specs=out_specs,
        out_shape=out_shape,
        scratch_shapes=[
            pltpu.VMEM((SSD_STATE, D_INNER), F32),
            pltpu.VMEM((KV_HEADS, BLOCK, LANES), BF16),
            pltpu.VMEM((KV_HEADS, LANES, BLOCK), BF16),
        ],
        compiler_params=_params("arbitrary", "arbitrary"),
        name="meta_mixers" if prefix else "token_mixers",
    )(*ssd_args, *att_args)


def _merge_kernel(h_ref, ys_ref, ya_ref, gl_ref, bg_ref, wsd_ref, wad_ref, wo_ref, o_ref):
    gl = gl_ref[...].astype(F32) + bg_ref[...]
    g_ssd = _sigmoid(gl[:, :D_MODEL])
    g_attn = _sigmoid(gl[:, D_MODEL:])
    merged = g_ssd * _dot(ys_ref[...], wsd_ref[...]) + g_attn * _dot(ya_ref[...], wad_ref[...])
    o_ref[...] = h_ref[...] + _dot(merged.astype(BF16), wo_ref[...])


def _merge(h, ys, ya, proj, bg, wsd, wad, wo, *, tm):
    t = h.shape[0]
    row = lambda w: pl.BlockSpec((tm, w), lambda i: (i, 0))
    const = lambda shape: pl.BlockSpec(shape, lambda i: (0, 0))
    return pl.pallas_call(
        _merge_kernel,
        grid=(t // tm,),
        in_specs=[
            row(D_MODEL), row(D_INNER), row(D_ATTN),
            pl.BlockSpec((tm, 2 * D_MODEL), lambda i: (i, COL_GATE // (2 * D_MODEL))),
            const((1, 2 * D_MODEL)),
            const((D_INNER, D_MODEL)), const((D_ATTN, D_MODEL)), const((D_MODEL, D_MODEL)),
        ],
        out_specs=row(D_MODEL),
        out_shape=jax.ShapeDtypeStruct((t, D_MODEL), F32),
        input_output_aliases={0: 0},
        compiler_params=_params("arbitrary"),
        name="merge_out_proj",
    )(h, ys, ya, proj, bg, wsd, wad, wo)


def _mlp_kernel(h_ref, g_ref, wup_ref, wdn_ref, o_ref, *, ff_chunk):
    x = h_ref[...]
    ms = jnp.mean(x * x, axis=-1, keepdims=True)
    xn = (x * lax.rsqrt(ms + EPS) * g_ref[...]).astype(BF16)
    acc = x
    for c in range(D_FF // ff_chunk):
        u = jnp.maximum(_dot(xn, wup_ref[:, c * ff_chunk:(c + 1) * ff_chunk]), 0.0)
        acc = acc + _dot((u * u).astype(BF16), wdn_ref[c * ff_chunk:(c + 1) * ff_chunk, :])
    o_ref[...] = acc


def _mlp(h, g, wup, wdn, *, tm, ff_chunk):
    t = h.shape[0]
    const = lambda shape: pl.BlockSpec(shape, lambda i: (0, 0))
    return pl.pallas_call(
        functools.partial(_mlp_kernel, ff_chunk=ff_chunk),
        grid=(t // tm,),
        in_specs=[
            pl.BlockSpec((tm, D_MODEL), lambda i: (i, 0)),
            const((1, D_MODEL)),
            const((D_MODEL, D_FF)),
            const((D_FF, D_MODEL)),
        ],
        out_specs=pl.BlockSpec((tm, D_MODEL), lambda i: (i, 0)),
        out_shape=jax.ShapeDtypeStruct((t, D_MODEL), F32),
        input_output_aliases={0: 0},
        compiler_params=_params("arbitrary"),
        name="mlp",
    )(h, g, wup, wdn)


def _rope_tables(pos):
    half = ATTN_HEAD_DIM // 2
    inv_freq = ROPE_THETA ** (-jnp.arange(half, dtype=F32) / half)
    ang = pos.astype(F32)[:, None] * inv_freq[None, :]
    cos = jnp.cos(ang)
    sin = jnp.sin(ang)
    cos_t = jnp.tile(cos, (1, LANES // half))
    sin_t = jnp.tile(jnp.concatenate([-sin, sin], axis=1), (1, LANES // ATTN_HEAD_DIM))
    return cos_t, sin_t


def _row_tile(t, target):
    tm = target
    while t % tm:
        tm //= 2
    return tm


def kernel(x, meta_tokens, norm1_g, w_in, b_gate, conv_w, conv_b, dt_bias, a_log, d_skip, ssd_norm_g,
           q_norm_g, k_norm_g, sinks, w_ssd_down, w_attn_down, w_o, norm2_g, w_mlp_up, w_mlp_down):
    bsz, seq, _ = x.shape
    depth = w_in.shape[0]
    assert seq % BLOCK == 0
    t = bsz * seq

    h = x.reshape(t, D_MODEL)
    h_meta = jnp.concatenate([jnp.zeros((PAD, D_MODEL), x.dtype), meta_tokens.astype(x.dtype)], axis=0)

    rope_meta = _rope_tables(jnp.arange(BLOCK) - PAD)
    rope_main = _rope_tables(jnp.arange(seq) + N_META)
    expand = (jnp.arange(LANES)[:, None] == (jnp.arange(D_INNER)[None, :] // SSD_HEAD_DIM)).astype(BF16)
    bd = (((jnp.arange(LANES)[:, None] // ATTN_HEAD_DIM)
           == (jnp.arange(LANES)[None, :] // ATTN_HEAD_DIM)) * (1.0 / ATTN_HEAD_DIM)).astype(BF16)

    ri = jnp.arange((CONV_WIDTH - 1) * BLOCK)
    src_row = BLOCK + ri % BLOCK - (ri // BLOCK + 1)
    shift_any = (jnp.arange(2 * BLOCK)[None, :] == src_row[:, None])
    shift_meta = jnp.logical_and(shift_any, jnp.arange(2 * BLOCK)[None, :] >= BLOCK + PAD).astype(BF16)
    shift_main = shift_any.astype(BF16)

    o_z, o_xbc, o_dt = 0, D_INNER, D_INNER + CONV_DIM
    o_q = o_dt + SSD_HEADS
    o_k, o_v, o_g = o_q + D_ATTN, o_q + D_ATTN + D_KV, o_q + D_ATTN + 2 * D_KV

    tm_proj = _row_tile(t, 1024)
    tm_rows = _row_tile(t, 512)

    for l in range(depth):
        w = w_in[l]
        w_main = jnp.concatenate(
            [w[:, o_xbc:o_dt], w[:, o_z:o_xbc], w[:, o_g:], w[:, o_q:o_k], w[:, o_k:o_v], w[:, o_v:o_g]],
            axis=1).astype(BF16)
        w_dt = jnp.pad(w[:, o_dt:o_q], ((0, 0), (0, LANES - SSD_HEADS))).astype(BF16)
        pad_heads = lambda v: jnp.pad(v, (0, LANES - SSD_HEADS))[None]
        mixer_params = (conv_w[l], conv_b[l][None], pad_heads(dt_bias[l]), pad_heads(a_log[l]),
                        jnp.repeat(d_skip[l], SSD_HEAD_DIM)[None], ssd_norm_g[l][None], expand, sinks[l])
        head_gains = (jnp.tile(q_norm_g[l], LANES // ATTN_HEAD_DIM)[None],
                      jnp.tile(k_norm_g[l], LANES // ATTN_HEAD_DIM)[None], bd)
        dense = (w_ssd_down[l].astype(BF16), w_attn_down[l].astype(BF16), w_o[l].astype(BF16))
        mlp_w = (norm2_g[l][None], w_mlp_up[l].astype(BF16), w_mlp_down[l].astype(BF16))

        proj_m, dt_m = _inproj(h_meta, norm1_g[l][None], w_main, w_dt, tm=BLOCK, tn=N_PROJ // 2)
        proj_m3 = proj_m.reshape(1, BLOCK, N_PROJ)
        ys_m, ya_m, state0, k_meta, v_meta = _mixer(proj_m3, dt_m.reshape(1, BLOCK, LANES), shift_meta, *mixer_params,
                                                    *rope_meta, *head_gains)
        h_meta = _merge(h_meta, ys_m.reshape(BLOCK, D_INNER), ya_m.reshape(BLOCK, D_ATTN), proj_m, b_gate[l][None],
                        *dense, tm=BLOCK)
        h_meta = _mlp(h_meta, *mlp_w, tm=BLOCK, ff_chunk=1024)

        proj, dt_raw = _inproj(h, norm1_g[l][None], w_main, w_dt, tm=tm_proj, tn=N_PROJ // 2)
        y_ssd, y_attn = _mixer(proj.reshape(bsz, seq, N_PROJ), dt_raw.reshape(bsz, seq, LANES), shift_main,
                               *mixer_params, *rope_main, *head_gains, carry=(proj_m3, state0, k_meta, v_meta))
        h = _merge(h, y_ssd.reshape(t, D_INNER), y_attn.reshape(t, D_ATTN), proj, b_gate[l][None], *dense, tm=tm_rows)
        h = _mlp(h, *mlp_w, tm=tm_rows, ff_chunk=1024)

    return h.reshape(bsz, seq, D_MODEL)
```

```python
import functools

import jax
import jax.numpy as jnp
from jax import lax
from jax.experimental import pallas as pl
from jax.experimental.pallas import tpu as pltpu

F32 = jnp.float32
BF16 = jnp.bfloat16

D_MODEL = 1024
N_META = 16
BLOCK = 128
EPS = 1e-6
D_INNER = 2048
SSD_HEAD_DIM = 64
SSD_HEADS = 32
SSD_GROUPS = 8
HEADS_PER_GROUP = 4
SSD_STATE = 128
CONV_WIDTH = 4
CONV_DIM = 4096
GROUP_WIDTH = D_INNER // SSD_GROUPS
ATTN_HEAD_DIM = 64
Q_HEADS = 16
KV_HEADS = 4
D_ATTN = 1024
D_KV = 256
ROPE_THETA = 10000.0
D_FF = 4096

PAD = BLOCK - N_META
LANES = 128

COL_XBC = 0
COL_Z = CONV_DIM
COL_GATE = COL_Z + D_INNER
COL_Q = COL_GATE + 2 * D_MODEL
COL_K = COL_Q + D_ATTN
COL_V = COL_K + D_KV
N_PROJ = COL_V + D_KV

PROJ_DTYPE = BF16
VMEM_LIMIT = 56 * 1024 * 1024
LOG2E = 1.4426950408889634
CONV_TAIL = 16


def _params(*sem):
    return pltpu.CompilerParams(dimension_semantics=sem, vmem_limit_bytes=VMEM_LIMIT)


def _sigmoid(x):
    return 1.0 / (1.0 + jnp.exp2(x * (-LOG2E)))


def _dot(a, b):
    return jnp.dot(a, b, preferred_element_type=F32)


def _dot_nt(a, b):
    return lax.dot_general(a, b, (((1,), (1,)), ((), ())), preferred_element_type=F32)


def _dot_hilo(a, b):
    hi = a.astype(BF16)
    lo = (a - hi.astype(F32)).astype(BF16)
    return _dot(hi, b) + _dot(lo, b)


def _inproj_kernel(h_ref, g_ref, w_ref, wdt_ref, proj_ref, dt_ref, xn_ref):
    @pl.when(pl.program_id(1) == 0)
    def _():
        x = h_ref[...]
        ms = jnp.mean(x * x, axis=-1, keepdims=True)
        xn = (x * lax.rsqrt(ms + EPS) * g_ref[...]).astype(BF16)
        xn_ref[...] = xn
        dt_ref[...] = _dot(xn, wdt_ref[...])

    proj_ref[...] = _dot(xn_ref[...], w_ref[...]).astype(proj_ref.dtype)


def _inproj(h, g, w, wdt, *, tm, tn):
    t = h.shape[0]
    return pl.pallas_call(
        _inproj_kernel,
        grid=(t // tm, N_PROJ // tn),
        in_specs=[
            pl.BlockSpec((tm, D_MODEL), lambda i, j: (i, 0)),
            pl.BlockSpec((1, D_MODEL), lambda i, j: (0, 0)),
            pl.BlockSpec((D_MODEL, tn), lambda i, j: (0, j)),
            pl.BlockSpec((D_MODEL, LANES), lambda i, j: (0, 0)),
        ],
        out_specs=[
            pl.BlockSpec((tm, tn), lambda i, j: (i, j)),
            pl.BlockSpec((tm, LANES), lambda i, j: (i, 0)),
        ],
        out_shape=[
            jax.ShapeDtypeStruct((t, N_PROJ), PROJ_DTYPE),
            jax.ShapeDtypeStruct((t, LANES), F32),
        ],
        scratch_shapes=[pltpu.VMEM((tm, D_MODEL), BF16)],
        compiler_params=_params("arbitrary", "arbitrary"),
        name="in_proj",
    )(h, g, w, wdt)


def _cumsum_rows(x):
    row = lax.broadcasted_iota(jnp.int32, x.shape, 0)
    shift = 1
    while shift < x.shape[0]:
        x = x + jnp.where(row >= shift, pltpu.roll(x, shift, axis=0), 0.0)
        shift *= 2
    return x


def _ssd_phases(prefix, xbc_ref, xprev_ref, xmeta_ref, state0_ref, z_ref, dt_ref, shift_ref, cw_ref, cb_ref, dtb_ref,
                alog_ref, dskip_ref, ng_ref, expand_ref, y_ref, state_out_ref, state_ref):
    n = pl.program_id(1)
    row = lax.broadcasted_iota(jnp.int32, (BLOCK, 1), 0)
    lane = lax.broadcasted_iota(jnp.int32, (BLOCK, LANES), 1)
    x_cur = xbc_ref[0]
    if prefix:
        live = jnp.logical_and(row >= PAD, lane < SSD_HEADS)
        state_ref[...] = jnp.zeros_like(state_ref)
        x_cat = jnp.concatenate([jnp.zeros_like(x_cur), x_cur], axis=0)
    else:
        live = lane < SSD_HEADS

        @pl.when(n == 0)
        def _():
            state_ref[...] = state0_ref[...]

        tail = jnp.where(n == 0, xmeta_ref[0, BLOCK - CONV_TAIL:, :], xprev_ref[0, BLOCK - CONV_TAIL:, :])
        x_cat = jnp.concatenate([jnp.zeros((BLOCK - CONV_TAIL, CONV_DIM), BF16), tail, x_cur], axis=0)

    shifted = _dot(shift_ref[...], x_cat)
    yield
    acc = cb_ref[...] + cw_ref[CONV_WIDTH - 1:CONV_WIDTH, :] * x_cur.astype(F32)
    for s in range(1, CONV_WIDTH):
        acc = acc + cw_ref[CONV_WIDTH - 1 - s:CONV_WIDTH - s, :] * shifted[(s - 1) * BLOCK:s * BLOCK]
    yield
    xc = acc * _sigmoid(acc)
    xs = xc[:, :D_INNER]
    yield

    dtr = dt_ref[0] + dtb_ref[...]
    dt = jnp.maximum(dtr, 0.0) + jnp.log1p(jnp.exp(-jnp.abs(dtr)))
    dt = jnp.where(live, dt, 0.0)
    a_neg = jnp.where(lane[0:1] < SSD_HEADS, -LOG2E * jnp.exp(alog_ref[...]), 0.0)
    acum = _cumsum_rows(dt * a_neg)
    atot = acum[BLOCK - 1:BLOCK, :]
    acum_t = acum.T

    per_head = jnp.concatenate([dt, jnp.exp2(acum), jnp.exp2(atot - acum)], axis=0)
    wide = _dot_hilo(per_head, expand_ref[...])
    dt_w = wide[0:BLOCK]
    ea_w = wide[BLOCK:2 * BLOCK]
    dec_w = wide[2 * BLOCK:3 * BLOCK]
    xdt = xs * dt_w
    xdt_b = xdt.astype(BF16)
    xw_b = (xdt * dec_w).astype(BF16)
    chunk_decay = ea_w[BLOCK - 1:BLOCK, :]
    yield

    li = lax.broadcasted_iota(jnp.int32, (BLOCK, BLOCK), 0)
    si = lax.broadcasted_iota(jnp.int32, (BLOCK, BLOCK), 1)
    tril = li >= si
    lane_lo = lane < SSD_HEAD_DIM

    y_parts = []
    for g in range(SSD_GROUPS):
        c0 = g * GROUP_WIDTH
        bg = xc[:, D_INNER + g * SSD_STATE:D_INNER + (g + 1) * SSD_STATE]
        cg = xc[:, D_INNER + SSD_GROUPS * SSD_STATE + g * SSD_STATE:
                D_INNER + SSD_GROUPS * SSD_STATE + (g + 1) * SSD_STATE].astype(BF16)
        bg_t = bg.T.astype(BF16)
        cb = _dot(cg, bg_t)
        st_old = state_ref[:, c0:c0 + GROUP_WIDTH]
        y_off = _dot(cg, st_old.astype(BF16)) * ea_w[:, c0:c0 + GROUP_WIDTH]
        st_new = _dot(bg_t, xw_b[:, c0:c0 + GROUP_WIDTH])
        state_ref[:, c0:c0 + GROUP_WIDTH] = st_old * chunk_decay[:, c0:c0 + GROUP_WIDTH] + st_new
        for pair in range(HEADS_PER_GROUP // 2):
            ms = []
            for r in range(2):
                hd = g * HEADS_PER_GROUP + 2 * pair + r
                diff = acum[:, hd:hd + 1] - acum_t[hd:hd + 1, :]
                lm = jnp.exp2(jnp.where(tril, diff, -jnp.inf))
                ms.append((cb * lm).astype(BF16))
            lhs = jnp.concatenate(ms, axis=1)
            xp = xdt_b[:, c0 + pair * LANES:c0 + (pair + 1) * LANES]
            zero = jnp.zeros_like(xp)
            rhs = jnp.concatenate([jnp.where(lane_lo, xp, zero), jnp.where(lane_lo, zero, xp)], axis=0)
            y_diag = _dot(lhs, rhs)
            y_parts.append(y_diag + y_off[:, pair * LANES:(pair + 1) * LANES])
        if g % 2 == 1:
            yield
    y = jnp.concatenate(y_parts, axis=1) + xs * dskip_ref[...]
    zf = z_ref[0].astype(F32)
    y = y * (zf * _sigmoid(zf))
    yield
    outs = []
    for g in range(SSD_GROUPS):
        yg = y[:, g * GROUP_WIDTH:(g + 1) * GROUP_WIDTH]
        ms = jnp.mean(yg * yg, axis=-1, keepdims=True)
        outs.append(yg * lax.rsqrt(ms + EPS))
    y_ref[0] = (jnp.concatenate(outs, axis=1) * ng_ref[...]).astype(y_ref.dtype)
    if prefix:
        state_out_ref[...] = state_ref[...]


def _attn_phases(prefix, sinks_ref, q_ref, k_ref, v_ref, cos_ref, sin_ref, qg_ref, kg_ref, bd_ref, kmeta_ref, vmeta_ref,
                 o_ref, kprev_ref, vprev_ref):
    n = pl.program_id(1) + (0 if prefix else 1)
    rep = Q_HEADS // KV_HEADS
    wide = rep * BLOCK
    lane = lax.broadcasted_iota(jnp.int32, (BLOCK, LANES), 1)
    row = lax.broadcasted_iota(jnp.int32, (BLOCK, LANES), 0)
    lane_lo = lane < ATTN_HEAD_DIM
    row_lo = row < ATTN_HEAD_DIM
    first_half = (lane % ATTN_HEAD_DIM) < (ATTN_HEAD_DIM // 2)
    cos = cos_ref[...]
    sin = sin_ref[...]
    bd = bd_ref[...]

    def norm_rope(t, gain):
        ms = _dot_hilo(t * t, bd)
        t = t * lax.rsqrt(ms + EPS) * gain
        partner = jnp.where(first_half, pltpu.roll(t, LANES - 32, axis=1), pltpu.roll(t, 32, axis=1))
        return t * cos + partner * sin

    kd_cur, vt_cur = [], []
    k_gain = kg_ref[...] * (ATTN_HEAD_DIM ** -0.5)
    for c in range(D_KV // LANES):
        csl = slice(c * LANES, (c + 1) * LANES)
        kr = norm_rope(k_ref[0, :, csl].astype(F32), k_gain)
        sw = pltpu.roll(kr, ATTN_HEAD_DIM, axis=1)
        kd_cur += [jnp.where(lane_lo, kr, sw).astype(BF16), jnp.where(lane_lo, sw, kr).astype(BF16)]
        vt = v_ref[0, :, csl].astype(F32).T
        lo, hi = vt[:ATTN_HEAD_DIM], vt[ATTN_HEAD_DIM:]
        vt_cur += [jnp.concatenate([lo, lo], axis=0).astype(BF16), jnp.concatenate([hi, hi], axis=0).astype(BF16)]
    yield

    @pl.when(n <= 1)
    def _():
        kprev_ref[...] = jnp.zeros_like(kprev_ref)
        vprev_ref[...] = jnp.zeros_like(vprev_ref)

    if prefix:
        for j in range(KV_HEADS):
            kmeta_ref[j] = kd_cur[j][PAD:, :]
            vmeta_ref[j] = vt_cur[j]

    neg = -jnp.inf
    key_i = lax.broadcasted_iota(jnp.int32, (BLOCK, wide), 0)
    qry_i = lax.broadcasted_iota(jnp.int32, (BLOCK, wide), 1) % BLOCK
    tri = key_i <= qry_i
    band_bias = jnp.where(tri, jnp.where(n >= 1, 0.0, neg), jnp.where(n >= 2, 0.0, neg))
    meta_key = lax.broadcasted_iota(jnp.int32, (N_META, wide), 0)
    meta_qry = lax.broadcasted_iota(jnp.int32, (N_META, wide), 1) % BLOCK
    meta_ok = jnp.logical_or(n >= 1, meta_key <= meta_qry - PAD)
    meta_bias = jnp.where(meta_ok, 0.0, neg)
    zeros_pad = jnp.zeros((PAD, wide), BF16)

    qs, sink = [], []
    for j in range(KV_HEADS):
        rows, sk = [], []
        for qc in range(2):
            col = (2 * j + qc) * LANES
            qr = norm_rope(q_ref[0, :, col:col + LANES].astype(F32), qg_ref[...])
            rows.append(jnp.where(lane_lo, qr, 0.0))
            rows.append(jnp.where(lane_lo, 0.0, qr))
            for r in range(2):
                sk.append(jnp.full((1, BLOCK), sinks_ref[rep * j + 2 * qc + r], F32))
        qs.append(jnp.concatenate(rows, axis=0).astype(BF16))
        if j % 2 == 1:
            yield
        sink.append(jnp.concatenate(sk, axis=1))
    s_cur = [_dot_nt(kd_cur[j], qs[j]) for j in range(KV_HEADS)]
    s_prev = [_dot_nt(kprev_ref[j], qs[j]) for j in range(KV_HEADS)]
    s_meta = [_dot_nt(kmeta_ref[j], qs[j]) + meta_bias for j in range(KV_HEADS)]
    s_band = [jnp.where(tri, s_cur[j], s_prev[j]) + band_bias for j in range(KV_HEADS)]
    yield
    m = [jnp.maximum(jnp.maximum(jnp.max(s_band[j], axis=0, keepdims=True),
                                 jnp.max(s_meta[j], axis=0, keepdims=True)), sink[j]) for j in range(KV_HEADS)]
    p_band = [jnp.exp(s_band[j] - m[j]) for j in range(KV_HEADS)]
    p_meta = [jnp.exp(s_meta[j] - m[j]) for j in range(KV_HEADS)]
    denom = [jnp.sum(p_band[j], axis=0, keepdims=True) + jnp.sum(p_meta[j], axis=0, keepdims=True)
             + jnp.exp(sink[j] - m[j]) for j in range(KV_HEADS)]
    yield
    p_cur = [jnp.where(tri, p_band[j], 0.0) for j in range(KV_HEADS)]
    p_all = [jnp.concatenate([p_cur[j].astype(BF16), (p_band[j] - p_cur[j]).astype(BF16), zeros_pad,
                              p_meta[j].astype(BF16)], axis=0) for j in range(KV_HEADS)]
    yield
    o_t = [_dot(jnp.concatenate([vt_cur[j], vprev_ref[j], vmeta_ref[j]], axis=1), p_all[j]) * (1.0 / denom[j])
           for j in range(KV_HEADS)]
    yield
    out_cols = []
    for j in range(KV_HEADS):
        for qc in range(2):
            pair = jnp.where(row_lo, o_t[j][:, (2 * qc) * BLOCK:(2 * qc + 1) * BLOCK],
                             o_t[j][:, (2 * qc + 1) * BLOCK:(2 * qc + 2) * BLOCK])
            out_cols.append(pair.T)
    o_ref[0] = jnp.concatenate(out_cols, axis=1).astype(o_ref.dtype)
    for j in range(KV_HEADS):
        kprev_ref[j] = kd_cur[j]
        vprev_ref[j] = vt_cur[j]


def _mixer_kernel(*refs, prefix):
    if prefix:
        (xbc_ref, z_ref, dt_ref, shift_ref, cw_ref, cb_ref, dtb_ref, alog_ref, dskip_ref, ng_ref, expand_ref,
         sinks_ref, q_ref, k_ref, v_ref, cos_ref, sin_ref, qg_ref, kg_ref, bd_ref,
         y_ref, o_ref, state_out_ref, kmeta_ref, vmeta_ref, state_ref, kprev_ref, vprev_ref) = refs
        xprev_ref = xmeta_ref = state0_ref = None
    else:
        (xbc_ref, xprev_ref, xmeta_ref, state0_ref, z_ref, dt_ref, shift_ref, cw_ref, cb_ref, dtb_ref, alog_ref,
         dskip_ref, ng_ref, expand_ref, sinks_ref, q_ref, k_ref, v_ref, cos_ref, sin_ref, qg_ref, kg_ref, bd_ref,
         kmeta_ref, vmeta_ref, y_ref, o_ref, state_ref, kprev_ref, vprev_ref) = refs
        state_out_ref = None
    ssd = _ssd_phases(prefix, xbc_ref, xprev_ref, xmeta_ref, state0_ref, z_ref, dt_ref, shift_ref, cw_ref, cb_ref,
                      dtb_ref, alog_ref, dskip_ref, ng_ref, expand_ref, y_ref, state_out_ref, state_ref)
    att = _attn_phases(prefix, sinks_ref, q_ref, k_ref, v_ref, cos_ref, sin_ref, qg_ref, kg_ref, bd_ref,
                       kmeta_ref, vmeta_ref, o_ref, kprev_ref, vprev_ref)
    live = [ssd, att]
    while live:
        for gen in list(live):
            if next(gen, "done") == "done":
                live.remove(gen)


def _mixer(proj, dt_raw, shift, cw, cb, dtb, alog, dskip, ng, expand, sinks, cos_t, sin_t, qg, kg, bd, carry=None):
    prefix = carry is None
    bsz, lp, _ = proj.shape
    const = lambda shape: pl.BlockSpec(shape, lambda b, n: (0,) * len(shape))
    blk = lambda width, col: pl.BlockSpec((1, BLOCK, width), lambda b, n: (b, n, col // width))
    kmeta_shape, vmeta_shape = (KV_HEADS, N_META, LANES), (KV_HEADS, LANES, BLOCK)
    ssd_in = [blk(CONV_DIM, COL_XBC)]
    ssd_args = [proj]
    if not prefix:
        xmeta, state0, kmeta, vmeta = carry
        ssd_in += [pl.BlockSpec((1, BLOCK, CONV_DIM), lambda b, n: (b, jnp.maximum(n - 1, 0), COL_XBC // CONV_DIM)),
                   pl.BlockSpec((1, BLOCK, CONV_DIM), lambda b, n: (0, 0, COL_XBC // CONV_DIM)),
                   const((SSD_STATE, D_INNER))]
        ssd_args += [proj, xmeta, state0]
    ssd_in += [blk(D_INNER, COL_Z), blk(LANES, 0), const(((CONV_WIDTH - 1) * BLOCK, 2 * BLOCK)),
               const((CONV_WIDTH, CONV_DIM)), const((1, CONV_DIM)), const((1, LANES)), const((1, LANES)),
               const((1, D_INNER)), const((1, D_INNER)), const((LANES, D_INNER))]
    ssd_args += [proj, dt_raw, shift, cw, cb, dtb, alog, dskip, ng, expand]
    att_in = [pl.BlockSpec(memory_space=pltpu.SMEM), blk(D_ATTN, COL_Q), blk(D_KV, COL_K), blk(D_KV, COL_V),
              pl.BlockSpec((BLOCK, LANES), lambda b, n: (n, 0)), pl.BlockSpec((BLOCK, LANES), lambda b, n: (n, 0)),
              const((1, LANES)), const((1, LANES)), const((LANES, LANES))]
    att_args = [sinks, proj, proj, proj, cos_t, sin_t, qg, kg, bd]
    out_specs = [pl.BlockSpec((1, BLOCK, D_INNER), lambda b, n: (b, n, 0)),
                 pl.BlockSpec((1, BLOCK, D_ATTN), lambda b, n: (b, n, 0))]
    out_shape = [jax.ShapeDtypeStruct((bsz, lp, D_INNER), BF16), jax.ShapeDtypeStruct((bsz, lp, D_ATTN), BF16)]
    if prefix:
        out_specs += [const((SSD_STATE, D_INNER)), const(kmeta_shape), const(vmeta_shape)]
        out_shape += [jax.ShapeDtypeStruct((SSD_STATE, D_INNER), F32), jax.ShapeDtypeStruct(kmeta_shape, BF16),
                      jax.ShapeDtypeStruct(vmeta_shape, BF16)]
    else:
        att_in += [const(kmeta_shape), const(vmeta_shape)]
        att_args += [kmeta, vmeta]
    return pl.pallas_call(
        functools.partial(_mixer_kernel, prefix=prefix),
        grid=(bsz, lp // BLOCK),
        in_specs=ssd_in + att_in,
        out_specs=out_specs,
        out_shape=out_shape,
        scratch_shapes=[
            pltpu.VMEM((SSD_STATE, D_INNER), F32),
            pltpu.VMEM((KV_HEADS, BLOCK, LANES), BF16),
            pltpu.VMEM((KV_HEADS, LANES, BLOCK), BF16),
        ],
        compiler_params=_params("arbitrary", "arbitrary"),
        name="meta_mixers" if prefix else "token_mixers",
    )(*ssd_args, *att_args)


def _merge_kernel(h_ref, ys_ref, ya_ref, gl_ref, bg_ref, wsd_ref, wad_ref, wo_ref, o_ref):
    gl = gl_ref[...].astype(F32) + bg_ref[...]
    g_ssd = _sigmoid(gl[:, :D_MODEL])
    g_attn = _sigmoid(gl[:, D_MODEL:])
    merged = g_ssd * _dot(ys_ref[...], wsd_ref[...]) + g_attn * _dot(ya_ref[...], wad_ref[...])
    o_ref[...] = h_ref[...] + _dot(merged.astype(BF16), wo_ref[...])


def _merge(h, ys, ya, proj, bg, wsd, wad, wo, *, tm):
    t = h.shape[0]
    row = lambda w: pl.BlockSpec((tm, w), lambda i: (i, 0))
    const = lambda shape: pl.BlockSpec(shape, lambda i: (0, 0))
    return pl.pallas_call(
        _merge_kernel,
        grid=(t // tm,),
        in_specs=[
            row(D_MODEL), row(D_INNER), row(D_ATTN),
            pl.BlockSpec((tm, 2 * D_MODEL), lambda i: (i, COL_GATE // (2 * D_MODEL))),
            const((1, 2 * D_MODEL)),
            const((D_INNER, D_MODEL)), const((D_ATTN, D_MODEL)), const((D_MODEL, D_MODEL)),
        ],
        out_specs=row(D_MODEL),
        out_shape=jax.ShapeDtypeStruct((t, D_MODEL), F32),
        compiler_params=_params("arbitrary"),
        name="merge_out_proj",
    )(h, ys, ya, proj, bg, wsd, wad, wo)


def _mlp_kernel(h_ref, g_ref, wup_ref, wdn_ref, o_ref, *, ff_chunk):
    x = h_ref[...]
    ms = jnp.mean(x * x, axis=-1, keepdims=True)
    xn = (x * lax.rsqrt(ms + EPS) * g_ref[...]).astype(BF16)
    acc = x
    for c in range(D_FF // ff_chunk):
        u = jnp.maximum(_dot(xn, wup_ref[:, c * ff_chunk:(c + 1) * ff_chunk]), 0.0)
        acc = acc + _dot((u * u).astype(BF16), wdn_ref[c * ff_chunk:(c + 1) * ff_chunk, :])
    o_ref[...] = acc


def _mlp(h, g, wup, wdn, *, tm, ff_chunk):
    t = h.shape[0]
    const = lambda shape: pl.BlockSpec(shape, lambda i: (0, 0))
    return pl.pallas_call(
        functools.partial(_mlp_kernel, ff_chunk=ff_chunk),
        grid=(t // tm,),
        in_specs=[
            pl.BlockSpec((tm, D_MODEL), lambda i: (i, 0)),
            const((1, D_MODEL)),
            const((D_MODEL, D_FF)),
            const((D_FF, D_MODEL)),
        ],
        out_specs=pl.BlockSpec((tm, D_MODEL), lambda i: (i, 0)),
        out_shape=jax.ShapeDtypeStruct((t, D_MODEL), F32),
        compiler_params=_params("arbitrary"),
        name="mlp",
    )(h, g, wup, wdn)


def _rope_tables(pos):
    half = ATTN_HEAD_DIM // 2
    inv_freq = ROPE_THETA ** (-jnp.arange(half, dtype=F32) / half)
    ang = pos.astype(F32)[:, None] * inv_freq[None, :]
    cos = jnp.cos(ang)
    sin = jnp.sin(ang)
    cos_t = jnp.tile(cos, (1, LANES // half))
    sin_t = jnp.tile(jnp.concatenate([-sin, sin], axis=1), (1, LANES // ATTN_HEAD_DIM))
    return cos_t, sin_t


def _row_tile(t, target):
    tm = target
    while t % tm:
        tm //= 2
    return tm


def kernel(x, meta_tokens, norm1_g, w_in, b_gate, conv_w, conv_b, dt_bias, a_log, d_skip, ssd_norm_g,
           q_norm_g, k_norm_g, sinks, w_ssd_down, w_attn_down, w_o, norm2_g, w_mlp_up, w_mlp_down):
    bsz, seq, _ = x.shape
    depth = w_in.shape[0]
    assert seq % BLOCK == 0
    t = bsz * seq

    h = x.reshape(t, D_MODEL)
    h_meta = jnp.concatenate([jnp.zeros((PAD, D_MODEL), x.dtype), meta_tokens.astype(x.dtype)], axis=0)

    rope_meta = _rope_tables(jnp.arange(BLOCK) - PAD)
    rope_main = _rope_tables(jnp.arange(seq) + N_META)
    expand = (jnp.arange(LANES)[:, None] == (jnp.arange(D_INNER)[None, :] // SSD_HEAD_DIM)).astype(BF16)
    bd = (((jnp.arange(LANES)[:, None] // ATTN_HEAD_DIM)
           == (jnp.arange(LANES)[None, :] // ATTN_HEAD_DIM)) * (1.0 / ATTN_HEAD_DIM)).astype(BF16)

    ri = jnp.arange((CONV_WIDTH - 1) * BLOCK)
    src_row = BLOCK + ri % BLOCK - (ri // BLOCK + 1)
    shift_any = (jnp.arange(2 * BLOCK)[None, :] == src_row[:, None])
    shift_meta = jnp.logical_and(shift_any, jnp.arange(2 * BLOCK)[None, :] >= BLOCK + PAD).astype(BF16)
    shift_main = shift_any.astype(BF16)

    o_z, o_xbc, o_dt = 0, D_INNER, D_INNER + CONV_DIM
    o_q = o_dt + SSD_HEADS
    o_k, o_v, o_g = o_q + D_ATTN, o_q + D_ATTN + D_KV, o_q + D_ATTN + 2 * D_KV

    tm_proj = _row_tile(t, 1024)
    tm_rows = _row_tile(t, 512)

    for l in range(depth):
        w = w_in[l]
        w_main = jnp.concatenate(
            [w[:, o_xbc:o_dt], w[:, o_z:o_xbc], w[:, o_g:], w[:, o_q:o_k], w[:, o_k:o_v], w[:, o_v:o_g]],
            axis=1).astype(BF16)
        w_dt = jnp.pad(w[:, o_dt:o_q], ((0, 0), (0, LANES - SSD_HEADS))).astype(BF16)
        pad_heads = lambda v: jnp.pad(v, (0, LANES - SSD_HEADS))[None]
        mixer_params = (conv_w[l], conv_b[l][None], pad_heads(dt_bias[l]), pad_heads(a_log[l]),
                        jnp.repeat(d_skip[l], SSD_HEAD_DIM)[None], ssd_norm_g[l][None], expand, sinks[l])
        head_gains = (jnp.tile(q_norm_g[l], LANES // ATTN_HEAD_DIM)[None],
                      jnp.tile(k_norm_g[l], LANES // ATTN_HEAD_DIM)[None], bd)
        dense = (w_ssd_down[l].astype(BF16), w_attn_down[l].astype(BF16), w_o[l].astype(BF16))
        mlp_w = (norm2_g[l][None], w_mlp_up[l].astype(BF16), w_mlp_down[l].astype(BF16))

        proj_m, dt_m = _inproj(h_meta, norm1_g[l][None], w_main, w_dt, tm=BLOCK, tn=N_PROJ // 2)
        proj_m3 = proj_m.reshape(1, BLOCK, N_PROJ)
        ys_m, ya_m, state0, k_meta, v_meta = _mixer(proj_m3, dt_m.reshape(1, BLOCK, LANES), shift_meta, *mixer_params,
                                                    *rope_meta, *head_gains)
        h_meta = _merge(h_meta, ys_m.reshape(BLOCK, D_INNER), ya_m.reshape(BLOCK, D_ATTN), proj_m, b_gate[l][None],
                        *dense, tm=BLOCK)
        h_meta = _mlp(h_meta, *mlp_w, tm=BLOCK, ff_chunk=1024)

        proj, dt_raw = _inproj(h, norm1_g[l][None], w_main, w_dt, tm=tm_proj, tn=N_PROJ // 2)
        y_ssd, y_attn = _mixer(proj.reshape(bsz, seq, N_PROJ), dt_raw.reshape(bsz, seq, LANES), shift_main,
                               *mixer_params, *rope_main, *head_gains, carry=(proj_m3, state0, k_meta, v_meta))
        h = _merge(h, y_ssd.reshape(t, D_INNER), y_attn.reshape(t, D_ATTN), proj, b_gate[l][None], *dense, tm=tm_rows)
        h = _mlp(h, *mlp_w, tm=tm_rows, ff_chunk=1024)

    return h.reshape(bsz, seq, D_MODEL)
```

```python
import functools

import jax
import jax.numpy as jnp
from jax import lax
from jax.experimental import pallas as pl
from jax.experimental.pallas import tpu as pltpu

F32 = jnp.float32
BF16 = jnp.bfloat16

D_MODEL = 1024
N_META = 16
BLOCK = 128
EPS = 1e-6
D_INNER = 2048
SSD_HEAD_DIM = 64
SSD_HEADS = 32
SSD_GROUPS = 8
HEADS_PER_GROUP = 4
SSD_STATE = 128
CONV_WIDTH = 4
CONV_DIM = 4096
GROUP_WIDTH = D_INNER // SSD_GROUPS
ATTN_HEAD_DIM = 64
Q_HEADS = 16
KV_HEADS = 4
D_ATTN = 1024
D_KV = 256
ROPE_THETA = 10000.0
D_FF = 4096

PAD = BLOCK - N_META
LANES = 128

COL_XBC = 0
COL_Z = CONV_DIM
COL_GATE = COL_Z + D_INNER
COL_Q = COL_GATE + 2 * D_MODEL
COL_K = COL_Q + D_ATTN
COL_V = COL_K + D_KV
N_PROJ = COL_V + D_KV

PROJ_DTYPE = BF16
VMEM_LIMIT = 56 * 1024 * 1024
LOG2E = 1.4426950408889634
CONV_TAIL = 16


def _params(*sem):
    return pltpu.CompilerParams(dimension_semantics=sem, vmem_limit_bytes=VMEM_LIMIT)


def _sigmoid(x):
    return 1.0 / (1.0 + jnp.exp2(x * (-LOG2E)))


def _dot(a, b):
    return jnp.dot(a, b, preferred_element_type=F32)


def _dot_nt(a, b):
    return lax.dot_general(a, b, (((1,), (1,)), ((), ())), preferred_element_type=F32)


def _dot_hilo(a, b):
    hi = a.astype(BF16)
    lo = (a - hi.astype(F32)).astype(BF16)
    return _dot(hi, b) + _dot(lo, b)


def _inproj_kernel(h_ref, g_ref, w_ref, wdt_ref, proj_ref, dt_ref, xn_ref):
    @pl.when(pl.program_id(1) == 0)
    def _():
        x = h_ref[...]
        ms = jnp.mean(x * x, axis=-1, keepdims=True)
        xn = (x * lax.rsqrt(ms + EPS) * g_ref[...]).astype(BF16)
        xn_ref[...] = xn
        dt_ref[...] = _dot(xn, wdt_ref[...])

    proj_ref[...] = _dot(xn_ref[...], w_ref[...]).astype(proj_ref.dtype)


def _inproj(h, g, w, wdt, *, tm, tn):
    t = h.shape[0]
    return pl.pallas_call(
        _inproj_kernel,
        grid=(t // tm, N_PROJ // tn),
        in_specs=[
            pl.BlockSpec((tm, D_MODEL), lambda i, j: (i, 0)),
            pl.BlockSpec((1, D_MODEL), lambda i, j: (0, 0)),
            pl.BlockSpec((D_MODEL, tn), lambda i, j: (0, j)),
            pl.BlockSpec((D_MODEL, LANES), lambda i, j: (0, 0)),
        ],
        out_specs=[
            pl.BlockSpec((tm, tn), lambda i, j: (i, j)),
            pl.BlockSpec((tm, LANES), lambda i, j: (i, 0)),
        ],
        out_shape=[
            jax.ShapeDtypeStruct((t, N_PROJ), PROJ_DTYPE),
            jax.ShapeDtypeStruct((t, LANES), F32),
        ],
        scratch_shapes=[pltpu.VMEM((tm, D_MODEL), BF16)],
        compiler_params=_params("arbitrary", "arbitrary"),
        name="in_proj",
    )(h, g, w, wdt)


def _cumsum_rows(x):
    row = lax.broadcasted_iota(jnp.int32, x.shape, 0)
    shift = 1
    while shift < x.shape[0]:
        x = x + jnp.where(row >= shift, pltpu.roll(x, shift, axis=0), 0.0)
        shift *= 2
    return x


def _ssd_phases(prefix, xbc_ref, xprev_ref, xmeta_ref, state0_ref, z_ref, dt_ref, shift_ref, cw_ref, cb_ref, dtb_ref,
                alog_ref, dskip_ref, ng_ref, expand_ref, y_ref, state_out_ref, state_ref):
    n = pl.program_id(1)
    row = lax.broadcasted_iota(jnp.int32, (BLOCK, 1), 0)
    lane = lax.broadcasted_iota(jnp.int32, (BLOCK, LANES), 1)
    x_cur = xbc_ref[0]
    if prefix:
        live = jnp.logical_and(row >= PAD, lane < SSD_HEADS)
        state_ref[...] = jnp.zeros_like(state_ref)
        x_cat = jnp.concatenate([jnp.zeros_like(x_cur), x_cur], axis=0)
    else:
        live = lane < SSD_HEADS

        @pl.when(n == 0)
        def _():
            state_ref[...] = state0_ref[...]

        tail = jnp.where(n == 0, xmeta_ref[0, BLOCK - CONV_TAIL:, :], xprev_ref[0, BLOCK - CONV_TAIL:, :])
        x_cat = jnp.concatenate([jnp.zeros((BLOCK - CONV_TAIL, CONV_DIM), BF16), tail, x_cur], axis=0)

    shifted = _dot(shift_ref[...], x_cat)
    yield
    acc = cb_ref[...] + cw_ref[CONV_WIDTH - 1:CONV_WIDTH, :] * x_cur.astype(F32)
    for s in range(1, CONV_WIDTH):
        acc = acc + cw_ref[CONV_WIDTH - 1 - s:CONV_WIDTH - s, :] * shifted[(s - 1) * BLOCK:s * BLOCK]
    yield
    xc = acc * _sigmoid(acc)
    xs = xc[:, :D_INNER]
    yield

    dtr = dt_ref[0] + dtb_ref[...]
    dt = jnp.maximum(dtr, 0.0) + jnp.log1p(jnp.exp(-jnp.abs(dtr)))
    dt = jnp.where(live, dt, 0.0)
    a_neg = jnp.where(lane[0:1] < SSD_HEADS, -LOG2E * jnp.exp(alog_ref[...]), 0.0)
    acum = _cumsum_rows(dt * a_neg)
    atot = acum[BLOCK - 1:BLOCK, :]
    acum_t = acum.T

    wide = _dot(jnp.concatenate([dt, jnp.exp2(atot - acum)], axis=0).astype(BF16), expand_ref[...])
    dt_w = wide[0:BLOCK]
    dec_w = wide[BLOCK:2 * BLOCK]
    ea_w = _dot_hilo(jnp.exp2(acum), expand_ref[...])
    xdt = xs * dt_w
    xdt_b = xdt.astype(BF16)
    xw_b = (xdt * dec_w).astype(BF16)
    chunk_decay = ea_w[BLOCK - 1:BLOCK, :]
    yield

    li = lax.broadcasted_iota(jnp.int32, (BLOCK, BLOCK), 0)
    si = lax.broadcasted_iota(jnp.int32, (BLOCK, BLOCK), 1)
    tril = li >= si
    lane_lo = lane < SSD_HEAD_DIM

    y_parts = []
    for g in range(SSD_GROUPS):
        c0 = g * GROUP_WIDTH
        bg = xc[:, D_INNER + g * SSD_STATE:D_INNER + (g + 1) * SSD_STATE]
        cg = xc[:, D_INNER + SSD_GROUPS * SSD_STATE + g * SSD_STATE:
                D_INNER + SSD_GROUPS * SSD_STATE + (g + 1) * SSD_STATE].astype(BF16)
        bg_t = bg.T.astype(BF16)
        cb = _dot(cg, bg_t)
        st_old = state_ref[:, c0:c0 + GROUP_WIDTH]
        y_off = _dot(cg, st_old.astype(BF16)) * ea_w[:, c0:c0 + GROUP_WIDTH]
        st_new = _dot(bg_t, xw_b[:, c0:c0 + GROUP_WIDTH])
        state_ref[:, c0:c0 + GROUP_WIDTH] = st_old * chunk_decay[:, c0:c0 + GROUP_WIDTH] + st_new
        for pair in range(HEADS_PER_GROUP // 2):
            ms = []
            for r in range(2):
                hd = g * HEADS_PER_GROUP + 2 * pair + r
                diff = acum[:, hd:hd + 1] - acum_t[hd:hd + 1, :]
                lm = jnp.exp2(jnp.where(tril, diff, -jnp.inf))
                ms.append((cb * lm).astype(BF16))
            lhs = jnp.concatenate(ms, axis=1)
            xp = xdt_b[:, c0 + pair * LANES:c0 + (pair + 1) * LANES]
            zero = jnp.zeros_like(xp)
            rhs = jnp.concatenate([jnp.where(lane_lo, xp, zero), jnp.where(lane_lo, zero, xp)], axis=0)
            y_diag = _dot(lhs, rhs)
            y_parts.append(y_diag + y_off[:, pair * LANES:(pair + 1) * LANES])
        if g % 2 == 1:
            yield
    y = jnp.concatenate(y_parts, axis=1) + xs * dskip_ref[...]
    zf = z_ref[0].astype(F32)
    y = y * (zf * _sigmoid(zf))
    yield
    outs = []
    for g in range(SSD_GROUPS):
        yg = y[:, g * GROUP_WIDTH:(g + 1) * GROUP_WIDTH]
        ms = jnp.mean(yg * yg, axis=-1, keepdims=True)
        outs.append(yg * lax.rsqrt(ms + EPS))
    y_ref[0] = (jnp.concatenate(outs, axis=1) * ng_ref[...]).astype(y_ref.dtype)
    if prefix:
        state_out_ref[...] = state_ref[...]


def _attn_phases(prefix, sinks_ref, q_ref, k_ref, v_ref, cos_ref, sin_ref, qg_ref, kg_ref, bd_ref, kmeta_ref, vmeta_ref,
                 o_ref, kprev_ref, vprev_ref):
    n = pl.program_id(1) + (0 if prefix else 1)
    rep = Q_HEADS // KV_HEADS
    wide = rep * BLOCK
    lane = lax.broadcasted_iota(jnp.int32, (BLOCK, LANES), 1)
    row = lax.broadcasted_iota(jnp.int32, (BLOCK, LANES), 0)
    lane_lo = lane < ATTN_HEAD_DIM
    row_lo = row < ATTN_HEAD_DIM
    first_half = (lane % ATTN_HEAD_DIM) < (ATTN_HEAD_DIM // 2)
    cos = cos_ref[...]
    sin = sin_ref[...]
    bd = bd_ref[...]

    def norm_rope(t, gain):
        ms = _dot_hilo(t * t, bd)
        t = t * lax.rsqrt(ms + EPS) * gain
        partner = jnp.where(first_half, pltpu.roll(t, LANES - 32, axis=1), pltpu.roll(t, 32, axis=1))
        return t * cos + partner * sin

    kd_cur, vt_cur = [], []
    k_gain = kg_ref[...] * (ATTN_HEAD_DIM ** -0.5)
    for c in range(D_KV // LANES):
        csl = slice(c * LANES, (c + 1) * LANES)
        kr = norm_rope(k_ref[0, :, csl].astype(F32), k_gain)
        sw = pltpu.roll(kr, ATTN_HEAD_DIM, axis=1)
        kd_cur += [jnp.where(lane_lo, kr, sw).astype(BF16), jnp.where(lane_lo, sw, kr).astype(BF16)]
        vt = v_ref[0, :, csl].astype(F32).T
        lo, hi = vt[:ATTN_HEAD_DIM], vt[ATTN_HEAD_DIM:]
        vt_cur += [jnp.concatenate([lo, lo], axis=0).astype(BF16), jnp.concatenate([hi, hi], axis=0).astype(BF16)]
    yield

    @pl.when(n <= 1)
    def _():
        kprev_ref[...] = jnp.zeros_like(kprev_ref)
        vprev_ref[...] = jnp.zeros_like(vprev_ref)

    if prefix:
        for j in range(KV_HEADS):
            kmeta_ref[j] = kd_cur[j][PAD:, :]
            vmeta_ref[j] = vt_cur[j]

    neg = -jnp.inf
    key_i = lax.broadcasted_iota(jnp.int32, (BLOCK, wide), 0)
    qry_i = lax.broadcasted_iota(jnp.int32, (BLOCK, wide), 1) % BLOCK
    tri = key_i <= qry_i
    band_bias = jnp.where(tri, jnp.where(n >= 1, 0.0, neg), jnp.where(n >= 2, 0.0, neg))
    meta_key = lax.broadcasted_iota(jnp.int32, (N_META, wide), 0)
    meta_qry = lax.broadcasted_iota(jnp.int32, (N_META, wide), 1) % BLOCK
    meta_ok = jnp.logical_or(n >= 1, meta_key <= meta_qry - PAD)
    meta_bias = jnp.where(meta_ok, 0.0, neg)
    zeros_pad = jnp.zeros((PAD, wide), BF16)

    qs, sink = [], []
    for j in range(KV_HEADS):
        rows, sk = [], []
        for qc in range(2):
            col = (2 * j + qc) * LANES
            qr = norm_rope(q_ref[0, :, col:col + LANES].astype(F32), qg_ref[...])
            rows.append(jnp.where(lane_lo, qr, 0.0))
            rows.append(jnp.where(lane_lo, 0.0, qr))
            for r in range(2):
                sk.append(jnp.full((1, BLOCK), sinks_ref[rep * j + 2 * qc + r], F32))
        qs.append(jnp.concatenate(rows, axis=0).astype(BF16))
        if j % 2 == 1:
            yield
        sink.append(jnp.concatenate(sk, axis=1))
    s_cur = [_dot_nt(kd_cur[j], qs[j]) for j in range(KV_HEADS)]
    s_prev = [_dot_nt(kprev_ref[j], qs[j]) for j in range(KV_HEADS)]
    s_meta = [_dot_nt(kmeta_ref[j], qs[j]) + meta_bias for j in range(KV_HEADS)]
    s_band = [jnp.where(tri, s_cur[j], s_prev[j]) + band_bias for j in range(KV_HEADS)]
    yield
    m = [jnp.maximum(jnp.maximum(jnp.max(s_band[j], axis=0, keepdims=True),
                                 jnp.max(s_meta[j], axis=0, keepdims=True)), sink[j]) for j in range(KV_HEADS)]
    p_band = [jnp.exp(s_band[j] - m[j]) for j in range(KV_HEADS)]
    p_meta = [jnp.exp(s_meta[j] - m[j]) for j in range(KV_HEADS)]
    denom = [jnp.sum(p_band[j], axis=0, keepdims=True) + jnp.sum(p_meta[j], axis=0, keepdims=True)
             + jnp.exp(sink[j] - m[j]) for j in range(KV_HEADS)]
    yield
    p_cur = [jnp.where(tri, p_band[j], 0.0) for j in range(KV_HEADS)]
    p_all = [jnp.concatenate([p_cur[j].astype(BF16), (p_band[j] - p_cur[j]).astype(BF16), zeros_pad,
                              p_meta[j].astype(BF16)], axis=0) for j in range(KV_HEADS)]
    yield
    o_t = [_dot(jnp.concatenate([vt_cur[j], vprev_ref[j], vmeta_ref[j]], axis=1), p_all[j]) * (1.0 / denom[j])
           for j in range(KV_HEADS)]
    yield
    out_cols = []
    for j in range(KV_HEADS):
        for qc in range(2):
            pair = jnp.where(row_lo, o_t[j][:, (2 * qc) * BLOCK:(2 * qc + 1) * BLOCK],
                             o_t[j][:, (2 * qc + 1) * BLOCK:(2 * qc + 2) * BLOCK])
            out_cols.append(pair.T)
    o_ref[0] = jnp.concatenate(out_cols, axis=1).astype(o_ref.dtype)
    for j in range(KV_HEADS):
        kprev_ref[j] = kd_cur[j]
        vprev_ref[j] = vt_cur[j]


def _mixer_kernel(*refs, prefix):
    if prefix:
        (xbc_ref, z_ref, dt_ref, shift_ref, cw_ref, cb_ref, dtb_ref, alog_ref, dskip_ref, ng_ref, expand_ref,
         sinks_ref, q_ref, k_ref, v_ref, cos_ref, sin_ref, qg_ref, kg_ref, bd_ref,
         y_ref, o_ref, state_out_ref, kmeta_ref, vmeta_ref, state_ref, kprev_ref, vprev_ref) = refs
        xprev_ref = xmeta_ref = state0_ref = None
    else:
        (xbc_ref, xprev_ref, xmeta_ref, state0_ref, z_ref, dt_ref, shift_ref, cw_ref, cb_ref, dtb_ref, alog_ref,
         dskip_ref, ng_ref, expand_ref, sinks_ref, q_ref, k_ref, v_ref, cos_ref, sin_ref, qg_ref, kg_ref, bd_ref,
         kmeta_ref, vmeta_ref, y_ref, o_ref, state_ref, kprev_ref, vprev_ref) = refs
        state_out_ref = None
    ssd = _ssd_phases(prefix, xbc_ref, xprev_ref, xmeta_ref, state0_ref, z_ref, dt_ref, shift_ref, cw_ref, cb_ref,
                      dtb_ref, alog_ref, dskip_ref, ng_ref, expand_ref, y_ref, state_out_ref, state_ref)
    att = _attn_phases(prefix, sinks_ref, q_ref, k_ref, v_ref, cos_ref, sin_ref, qg_ref, kg_ref, bd_ref,
                       kmeta_ref, vmeta_ref, o_ref, kprev_ref, vprev_ref)
    live = [ssd, att]
    while live:
        for gen in list(live):
            if next(gen, "done") == "done":
                live.remove(gen)


def _mixer(proj, dt_raw, shift, cw, cb, dtb, alog, dskip, ng, expand, sinks, cos_t, sin_t, qg, kg, bd, carry=None):
    prefix = carry is None
    bsz, lp, _ = proj.shape
    const = lambda shape: pl.BlockSpec(shape, lambda b, n: (0,) * len(shape))
    blk = lambda width, col: pl.BlockSpec((1, BLOCK, width), lambda b, n: (b, n, col // width))
    kmeta_shape, vmeta_shape = (KV_HEADS, N_META, LANES), (KV_HEADS, LANES, BLOCK)
    ssd_in = [blk(CONV_DIM, COL_XBC)]
    ssd_args = [proj]
    if not prefix:
        xmeta, state0, kmeta, vmeta = carry
        ssd_in += [pl.BlockSpec((1, BLOCK, CONV_DIM), lambda b, n: (b, jnp.maximum(n - 1, 0), COL_XBC // CONV_DIM)),
                   pl.BlockSpec((1, BLOCK, CONV_DIM), lambda b, n: (0, 0, COL_XBC // CONV_DIM)),
                   const((SSD_STATE, D_INNER))]
        ssd_args += [proj, xmeta, state0]
    ssd_in += [blk(D_INNER, COL_Z), blk(LANES, 0), const(((CONV_WIDTH - 1) * BLOCK, 2 * BLOCK)),
               const((CONV_WIDTH, CONV_DIM)), const((1, CONV_DIM)), const((1, LANES)), const((1, LANES)),
               const((1, D_INNER)), const((1, D_INNER)), const((LANES, D_INNER))]
    ssd_args += [proj, dt_raw, shift, cw, cb, dtb, alog, dskip, ng, expand]
    att_in = [pl.BlockSpec(memory_space=pltpu.SMEM), blk(D_ATTN, COL_Q), blk(D_KV, COL_K), blk(D_KV, COL_V),
              pl.BlockSpec((BLOCK, LANES), lambda b, n: (n, 0)), pl.BlockSpec((BLOCK, LANES), lambda b, n: (n, 0)),
              const((1, LANES)), const((1, LANES)), const((LANES, LANES))]
    att_args = [sinks, proj, proj, proj, cos_t, sin_t, qg, kg, bd]
    out_specs = [pl.BlockSpec((1, BLOCK, D_INNER), lambda b, n: (b, n, 0)),
                 pl.BlockSpec((1, BLOCK, D_ATTN), lambda b, n: (b, n, 0))]
    out_shape = [jax.ShapeDtypeStruct((bsz, lp, D_INNER), BF16), jax.ShapeDtypeStruct((bsz, lp, D_ATTN), BF16)]
    if prefix:
        out_specs += [const((SSD_STATE, D_INNER)), const(kmeta_shape), const(vmeta_shape)]
        out_shape += [jax.ShapeDtypeStruct((SSD_STATE, D_INNER), F32), jax.ShapeDtypeStruct(kmeta_shape, BF16),
                      jax.ShapeDtypeStruct(vmeta_shape, BF16)]
    else:
        att_in += [const(kmeta_shape), const(vmeta_shape)]
        att_args += [kmeta, vmeta]
    return pl.pallas_call(
        functools.partial(_mixer_kernel, prefix=prefix),
        grid=(bsz, lp // BLOCK),
        in_specs=ssd_in + att_in,
        out_specs=out_specs,
        out_shape=out_shape,
        scratch_shapes=[
            pltpu.VMEM((SSD_STATE, D_INNER), F32),
            pltpu.VMEM((KV_HEADS, BLOCK, LANES), BF16),
            pltpu.VMEM((KV_HEADS, LANES, BLOCK), BF16),
        ],
        compiler_params=_params("arbitrary", "arbitrary"),
        name="meta_mixers" if prefix else "token_mixers",
    )(*ssd_args, *att_args)


def _merge_kernel(h_ref, ys_ref, ya_ref, gl_ref, bg_ref, wsd_ref, wad_ref, wo_ref, o_ref):
    gl = gl_ref[...].astype(F32) + bg_ref[...]
    g_ssd = _sigmoid(gl[:, :D_MODEL])
    g_attn = _sigmoid(gl[:, D_MODEL:])
    merged = g_ssd * _dot(ys_ref[...], wsd_ref[...]) + g_attn * _dot(ya_ref[...], wad_ref[...])
    o_ref[...] = h_ref[...] + _dot(merged.astype(BF16), wo_ref[...])


def _merge(h, ys, ya, proj, bg, wsd, wad, wo, *, tm):
    t = h.shape[0]
    row = lambda w: pl.BlockSpec((tm, w), lambda i: (i, 0))
    const = lambda shape: pl.BlockSpec(shape, lambda i: (0, 0))
    return pl.pallas_call(
        _merge_kernel,
        grid=(t // tm,),
        in_specs=[
            row(D_MODEL), row(D_INNER), row(D_ATTN),
            pl.BlockSpec((tm, 2 * D_MODEL), lambda i: (i, COL_GATE // (2 * D_MODEL))),
            const((1, 2 * D_MODEL)),
            const((D_INNER, D_MODEL)), const((D_ATTN, D_MODEL)), const((D_MODEL, D_MODEL)),
        ],
        out_specs=row(D_MODEL),
        out_shape=jax.ShapeDtypeStruct((t, D_MODEL), F32),
        compiler_params=_params("arbitrary"),
        name="merge_out_proj",
    )(h, ys, ya, proj, bg, wsd, wad, wo)


def _mlp_kernel(h_ref, g_ref, wup_ref, wdn_ref, o_ref, *, ff_chunk):
    x = h_ref[...]
    ms = jnp.mean(x * x, axis=-1, keepdims=True)
    xn = (x * lax.rsqrt(ms + EPS) * g_ref[...]).astype(BF16)
    acc = x
    for c in range(D_FF // ff_chunk):
        u = jnp.maximum(_dot(xn, wup_ref[:, c * ff_chunk:(c + 1) * ff_chunk]), 0.0)
        acc = acc + _dot((u * u).astype(BF16), wdn_ref[c * ff_chunk:(c + 1) * ff_chunk, :])
    o_ref[...] = acc


def _mlp(h, g, wup, wdn, *, tm, ff_chunk):
    t = h.shape[0]
    const = lambda shape: pl.BlockSpec(shape, lambda i: (0, 0))
    return pl.pallas_call(
        functools.partial(_mlp_kernel, ff_chunk=ff_chunk),
        grid=(t // tm,),
        in_specs=[
            pl.BlockSpec((tm, D_MODEL), lambda i: (i, 0)),
            const((1, D_MODEL)),
            const((D_MODEL, D_FF)),
            const((D_FF, D_MODEL)),
        ],
        out_specs=pl.BlockSpec((tm, D_MODEL), lambda i: (i, 0)),
        out_shape=jax.ShapeDtypeStruct((t, D_MODEL), F32),
        compiler_params=_params("arbitrary"),
        name="mlp",
    )(h, g, wup, wdn)


def _rope_tables(pos):
    half = ATTN_HEAD_DIM // 2
    inv_freq = ROPE_THETA ** (-jnp.arange(half, dtype=F32) / half)
    ang = pos.astype(F32)[:, None] * inv_freq[None, :]
    cos = jnp.cos(ang)
    sin = jnp.sin(ang)
    cos_t = jnp.tile(cos, (1, LANES // half))
    sin_t = jnp.tile(jnp.concatenate([-sin, sin], axis=1), (1, LANES // ATTN_HEAD_DIM))
    return cos_t, sin_t


def _row_tile(t, target):
    tm = target
    while t % tm:
        tm //= 2
    return tm


def kernel(x, meta_tokens, norm1_g, w_in, b_gate, conv_w, conv_b, dt_bias, a_log, d_skip, ssd_norm_g,
           q_norm_g, k_norm_g, sinks, w_ssd_down, w_attn_down, w_o, norm2_g, w_mlp_up, w_mlp_down):
    bsz, seq, _ = x.shape
    depth = w_in.shape[0]
    assert seq % BLOCK == 0
    t = bsz * seq

    h = x.reshape(t, D_MODEL)
    h_meta = jnp.concatenate([jnp.zeros((PAD, D_MODEL), x.dtype), meta_tokens.astype(x.dtype)], axis=0)

    rope_meta = _rope_tables(jnp.arange(BLOCK) - PAD)
    rope_main = _rope_tables(jnp.arange(seq) + N_META)
    expand = (jnp.arange(LANES)[:, None] == (jnp.arange(D_INNER)[None, :] // SSD_HEAD_DIM)).astype(BF16)
    bd = (((jnp.arange(LANES)[:, None] // ATTN_HEAD_DIM)
           == (jnp.arange(LANES)[None, :] // ATTN_HEAD_DIM)) * (1.0 / ATTN_HEAD_DIM)).astype(BF16)

    ri = jnp.arange((CONV_WIDTH - 1) * BLOCK)
    src_row = BLOCK + ri % BLOCK - (ri // BLOCK + 1)
    shift_any = (jnp.arange(2 * BLOCK)[None, :] == src_row[:, None])
    shift_meta = jnp.logical_and(shift_any, jnp.arange(2 * BLOCK)[None, :] >= BLOCK + PAD).astype(BF16)
    shift_main = shift_any.astype(BF16)

    o_z, o_xbc, o_dt = 0, D_INNER, D_INNER + CONV_DIM
    o_q = o_dt + SSD_HEADS
    o_k, o_v, o_g = o_q + D_ATTN, o_q + D_ATTN + D_KV, o_q + D_ATTN + 2 * D_KV

    tm_proj = _row_tile(t, 1024)
    tm_rows = _row_tile(t, 512)

    for l in range(depth):
        w = w_in[l]
        w_main = jnp.concatenate(
            [w[:, o_xbc:o_dt], w[:, o_z:o_xbc], w[:, o_g:], w[:, o_q:o_k], w[:, o_k:o_v], w[:, o_v:o_g]],
            axis=1).astype(BF16)
        w_dt = jnp.pad(w[:, o_dt:o_q], ((0, 0), (0, LANES - SSD_HEADS))).astype(BF16)
        pad_heads = lambda v: jnp.pad(v, (0, LANES - SSD_HEADS))[None]
        mixer_params = (conv_w[l], conv_b[l][None], pad_heads(dt_bias[l]), pad_heads(a_log[l]),
                        jnp.repeat(d_skip[l], SSD_HEAD_DIM)[None], ssd_norm_g[l][None], expand, sinks[l])
        head_gains = (jnp.tile(q_norm_g[l], LANES // ATTN_HEAD_DIM)[None],
                      jnp.tile(k_norm_g[l], LANES // ATTN_HEAD_DIM)[None], bd)
        dense = (w_ssd_down[l].astype(BF16), w_attn_down[l].astype(BF16), w_o[l].astype(BF16))
        mlp_w = (norm2_g[l][None], w_mlp_up[l].astype(BF16), w_mlp_down[l].astype(BF16))

        proj_m, dt_m = _inproj(h_meta, norm1_g[l][None], w_main, w_dt, tm=BLOCK, tn=N_PROJ // 2)
        proj_m3 = proj_m.reshape(1, BLOCK, N_PROJ)
        ys_m, ya_m, state0, k_meta, v_meta = _mixer(proj_m3, dt_m.reshape(1, BLOCK, LANES), shift_meta, *mixer_params,
                                                    *rope_meta, *head_gains)
        h_meta = _merge(h_meta, ys_m.reshape(BLOCK, D_INNER), ya_m.reshape(BLOCK, D_ATTN), proj_m, b_gate[l][None],
                        *dense, tm=BLOCK)
        h_meta = _mlp(h_meta, *mlp_w, tm=BLOCK, ff_chunk=1024)

        proj, dt_raw = _inproj(h, norm1_g[l][None], w_main, w_dt, tm=tm_proj, tn=N_PROJ // 2)
        y_ssd, y_attn = _mixer(proj.reshape(bsz, seq, N_PROJ), dt_raw.reshape(bsz, seq, LANES), shift_main,
                               *mixer_params, *rope_main, *head_gains, carry=(proj_m3, state0, k_meta, v_meta))
        h = _merge(h, y_ssd.reshape(t, D_INNER), y_attn.reshape(t, D_ATTN), proj, b_gate[l][None], *dense, tm=tm_rows)
        h = _mlp(h, *mlp_w, tm=tm_rows, ff_chunk=1024)

    return h.reshape(bsz, seq, D_MODEL)
```
